```python
import jax, jax.numpy as jnp
from jax import lax
import numpy as np

D_MODEL = 2048
BATCH = 2
SEQ = 4096
DEPTH = 1
DEC_BATCH = 32
DEC_SEQ = 4
PAST_LEN = 8192
PAGE_SIZE = 128

N_HEADS = 8
HEAD_DIM = 128
ATTN_WIDTH = N_HEADS * HEAD_DIM
ROT_DIM = HEAD_DIM // 4
ROPE_THETA = 500000.0
IDX_HEADS = 16
IDX_DIM = 64
IDX_ROT_DIM = IDX_DIM // 4
TOPK_MAX = 256
Q_BLOCK = 128
POOL_WINDOWS = (2, 4, 8, 16)
N_POOL_GROUPS = len(POOL_WINDOWS)
POOL_WIDTH = D_MODEL - ATTN_WIDTH
POOL_GROUP_DIM = POOL_WIDTH // N_POOL_GROUPS
POOL_STATE = max(POOL_WINDOWS) - 1
MIX_WIDTH = ATTN_WIDTH + POOL_WIDTH
IN_SPLITS = (ATTN_WIDTH, ATTN_WIDTH, ATTN_WIDTH, ATTN_WIDTH,
             IDX_HEADS * IDX_DIM, IDX_DIM, IDX_HEADS, POOL_WIDTH, POOL_WIDTH)
IN_COLS = sum(IN_SPLITS)
DEEPNORM_ALPHA = (2 * DEPTH) ** 0.25
DEEPNORM_BETA = (8 * DEPTH) ** -0.25
LN_EPS = 1e-5

kernel_name = 'hybrid_dsa_pool_decode_step'


def _rope(x, pos, rot):
    half = rot // 2
    freqs = ROPE_THETA ** (-jnp.arange(half, dtype=jnp.float32) * (2.0 / rot))
    ang = pos.astype(jnp.float32)[:, None] * freqs[None, :]
    shape = (1, pos.shape[0]) + (1,) * (x.ndim - 3) + (half,)
    cos = jnp.cos(ang).reshape(shape)
    sin = jnp.sin(ang).reshape(shape)
    xr = x[..., :rot].astype(jnp.float32)
    x1, x2 = xr[..., :half], xr[..., half:]
    rotated = jnp.concatenate([x1 * cos - x2 * sin, x1 * sin + x2 * cos], axis=-1).astype(x.dtype)
    return jnp.concatenate([rotated, x[..., rot:]], axis=-1)


def _project(x, pos, w_in):
    B, T, _ = x.shape
    h = jnp.einsum('btd,dc->btc', x, w_in)
    pts = np.cumsum(IN_SPLITS)[:-1].tolist()
    q, k, v, ga, qi, ki, wi, u, gb = jnp.split(h, pts, axis=-1)
    q = _rope(q.reshape(B, T, N_HEADS, HEAD_DIM), pos, ROT_DIM)
    k = _rope(k.reshape(B, T, N_HEADS, HEAD_DIM), pos, ROT_DIM)
    v = v.reshape(B, T, N_HEADS, HEAD_DIM)
    qi = _rope(qi.reshape(B, T, IDX_HEADS, IDX_DIM), pos, IDX_ROT_DIM)
    ki = _rope(ki, pos, IDX_ROT_DIM)
    return q, k, v, ga, qi, ki, wi, u, gb


def _index_scores(qi, wi, ki):
    s = jnp.einsum('bthd,bsd->bths', qi.astype(jnp.float32), ki.astype(jnp.float32))
    scale = (IDX_DIM ** -0.5) * (IDX_HEADS ** -0.5)
    return jnp.einsum('bths,bth->bts', jax.nn.relu(s), wi.astype(jnp.float32)) * scale


def _sparse_attend(q, k_sel, v_sel, valid):
    s = jnp.einsum('bthd,btkhd->bthk', q.astype(jnp.float32), k_sel.astype(jnp.float32)) * (HEAD_DIM ** -0.5)
    s = jnp.where(valid[:, :, None, :], s, -jnp.inf)
    p = jax.nn.softmax(s, axis=-1)
    o = jnp.einsum('bthk,btkhd->bthd', p, v_sel.astype(jnp.float32))
    return o.astype(q.dtype)


def _prompt_attention(q, k, v, qi, ki, wi):
    B, S = q.shape[0], q.shape[1]
    topk = min(TOPK_MAX, S // 4)
    n_blocks = S // Q_BLOCK
    key_pos = jnp.arange(S)
    bidx = jnp.arange(B)[:, None, None]

    def block(i):
        start = i * Q_BLOCK
        qb = lax.dynamic_slice_in_dim(q, start, Q_BLOCK, axis=1)
        qib = lax.dynamic_slice_in_dim(qi, start, Q_BLOCK, axis=1)
        wib = lax.dynamic_slice_in_dim(wi, start, Q_BLOCK, axis=1)
        qpos = start + jnp.arange(Q_BLOCK)
        scores = _index_scores(qib, wib, ki)
        causal = key_pos[None, :] <= qpos[:, None]
        scores = jnp.where(causal[None], scores, -jnp.inf)
        _, idx = lax.top_k(scores, topk)
        valid = idx <= qpos[None, :, None]
        return _sparse_attend(qb, k[bidx, idx], v[bidx, idx], valid)

    out = lax.map(block, jnp.arange(n_blocks))
    return out.transpose(1, 0, 2, 3, 4).reshape(B, S, N_HEADS, HEAD_DIM)


def _sample_attention(l, q, k_new, v_new, qi, ki_new, wi, cache_k, cache_v, cache_kidx, page_table):
    Bd, T = q.shape[0], q.shape[1]
    past = page_table.shape[1] * PAGE_SIZE
    L = past + T
    topk = min(TOPK_MAX, L // 4)
    ki_past = cache_kidx[l, page_table].reshape(Bd, past, IDX_DIM).astype(ki_new.dtype)
    ki_all = jnp.concatenate([ki_past, ki_new], axis=1)
    scores = _index_scores(qi, wi, ki_all)
    qpos = past + jnp.arange(T)
    causal = jnp.arange(L)[None, :] <= qpos[:, None]
    scores = jnp.where(causal[None], scores, -jnp.inf)
    _, idx = lax.top_k(scores, topk)
    valid = idx <= qpos[None, :, None]
    is_past = (idx < past)[..., None, None]
    bidx = jnp.arange(Bd)[:, None, None]
    pidx = jnp.minimum(idx, past - 1)
    phys_page = page_table[bidx, pidx // PAGE_SIZE]
    offset = pidx % PAGE_SIZE
    nidx = jnp.clip(idx - past, 0, T - 1)
    k_sel = jnp.where(is_past, cache_k[l, phys_page, offset].astype(k_new.dtype), k_new[bidx, nidx])
    v_sel = jnp.where(is_past, cache_v[l, phys_page, offset].astype(v_new.dtype), v_new[bidx, nidx])
    return _sparse_attend(q, k_sel, v_sel, valid)


def _pool_mix(u_ext, n_hist, w_pool, pool_scale):
    B, R, _ = u_ext.shape
    ug = u_ext.astype(jnp.float32).reshape(B, R, N_POOL_GROUPS, POOL_GROUP_DIM)
    cs = jnp.concatenate([jnp.zeros((B, 1, N_POOL_GROUPS, POOL_GROUP_DIM), jnp.float32),
                          jnp.cumsum(ug, axis=1)], axis=1)
    rows = np.arange(n_hist, R)
    outs = []
    for g, w in enumerate(POOL_WINDOWS):
        starts = np.maximum(rows + 1 - w, 0)
        count = jnp.asarray((rows + 1 - starts).astype(np.float32))
        cs_g = cs[:, :, g]
        mean = (cs_g[:, rows + 1] - cs_g[:, starts]) / count[None, :, None]
        outs.append(mean - ug[:, n_hist:, g])
    pooled = jnp.stack(outs, axis=2)
    mixed = jnp.einsum('btgc,gcd->btgd', pooled, w_pool.astype(jnp.float32))
    T = R - n_hist
    return (mixed.reshape(B, T, POOL_WIDTH) * pool_scale.astype(jnp.float32)).astype(u_ext.dtype)


def _combine(x, a, p, ga, gb, w_out, ln_g, ln_b):
    B, T, _ = x.shape
    mixed = jnp.concatenate([a.reshape(B, T, ATTN_WIDTH) * jax.nn.silu(ga),
                             p * jax.nn.silu(gb)], axis=-1)
    h = (DEEPNORM_ALPHA * x + jnp.einsum('btc,cd->btd', mixed, w_out)).astype(jnp.float32)
    mu = jnp.mean(h, axis=-1, keepdims=True)
    var = jnp.mean(jnp.square(h - mu), axis=-1, keepdims=True)
    y = (h - mu) * lax.rsqrt(var + LN_EPS) * ln_g.astype(jnp.float32) + ln_b.astype(jnp.float32)
    return y.astype(x.dtype)


def setup_inputs(seed: int = 0) -> dict:
    key = jax.random.key(seed)
    ks = jax.random.split(key, 13)
    n_pages = PAST_LEN // PAGE_SIZE
    n_pool = (DEC_BATCH * n_pages * 5) // 4
    f32 = jnp.float32
    x_prompt = jax.random.normal(ks[0], (BATCH, SEQ, D_MODEL), f32)
    x_sample = jax.random.normal(ks[1], (DEC_BATCH, DEC_SEQ, D_MODEL), f32)
    cache_k = jax.random.normal(ks[2], (DEPTH, n_pool, PAGE_SIZE, N_HEADS, HEAD_DIM), f32)
    cache_v = jax.random.normal(ks[3], (DEPTH, n_pool, PAGE_SIZE, N_HEADS, HEAD_DIM), f32)
    cache_kidx = jax.random.normal(ks[4], (DEPTH, n_pool, PAGE_SIZE, IDX_DIM), f32)
    state_pool = jax.random.normal(ks[5], (DEPTH, DEC_BATCH, POOL_STATE, POOL_WIDTH), f32)
    page_table = jax.random.permutation(ks[6], n_pool)[:DEC_BATCH * n_pages].reshape(DEC_BATCH, n_pages).astype(jnp.int32)
    col_scale = np.ones((IN_COLS,), np.float32)
    col_scale[2 * ATTN_WIDTH:3 * ATTN_WIDTH] = DEEPNORM_BETA
    w_in = jax.random.normal(ks[7], (DEPTH, D_MODEL, IN_COLS), f32) * (D_MODEL ** -0.5) * jnp.asarray(col_scale)
    w_pool = jax.random.normal(ks[8], (DEPTH, N_POOL_GROUPS, POOL_GROUP_DIM, POOL_GROUP_DIM), f32) * (POOL_GROUP_DIM ** -0.5)
    pool_scale = 1.0 + 0.1 * jax.random.normal(ks[9], (DEPTH, POOL_WIDTH), f32)
    w_out = jax.random.normal(ks[10], (DEPTH, MIX_WIDTH, D_MODEL), f32) * (MIX_WIDTH ** -0.5) * DEEPNORM_BETA
    ln_g = 1.0 + 0.02 * jax.random.normal(ks[11], (DEPTH, D_MODEL), f32)
    ln_b = 0.02 * jax.random.normal(ks[12], (DEPTH, D_MODEL), f32)
    return {'x_prompt': x_prompt, 'x_sample': x_sample, 'cache_k': cache_k, 'cache_v': cache_v,
            'cache_kidx': cache_kidx, 'state_pool': state_pool, 'page_table': page_table,
            'w_in': w_in, 'w_pool': w_pool, 'pool_scale': pool_scale, 'w_out': w_out,
            'ln_g': ln_g, 'ln_b': ln_b}


def reference(x_prompt, x_sample, cache_k, cache_v, cache_kidx, state_pool, page_table,
              w_in, w_pool, pool_scale, w_out, ln_g, ln_b):
    xp, xs = x_prompt, x_sample
    S, T = xp.shape[1], xs.shape[1]
    past = page_table.shape[1] * PAGE_SIZE
    pos_p = jnp.arange(S)
    pos_s = past + jnp.arange(T)
    pk, pv, pki, ppool, sk, sv, ski, spool = [], [], [], [], [], [], [], []
    for l in range(DEPTH):
        q, k, v, ga, qi, ki, wi, u, gb = _project(xp, pos_p, w_in[l])
        a = _prompt_attention(q, k, v, qi, ki, wi)
        p = _pool_mix(u, 0, w_pool[l], pool_scale[l])
        xp = _combine(xp, a, p, ga, gb, w_out[l], ln_g[l], ln_b[l])
        pk.append(k); pv.append(v); pki.append(ki); ppool.append(u[:, -POOL_STATE:])
        q, k, v, ga, qi, ki, wi, u, gb = _project(xs, pos_s, w_in[l])
        a = _sample_attention(l, q, k, v, qi, ki, wi, cache_k, cache_v, cache_kidx, page_table)
        u_ext = jnp.concatenate([state_pool[l].astype(u.dtype), u], axis=1)
        p = _pool_mix(u_ext, POOL_STATE, w_pool[l], pool_scale[l])
        xs = _combine(xs, a, p, ga, gb, w_out[l], ln_g[l], ln_b[l])
        sk.append(k); sv.append(v); ski.append(ki); spool.append(u_ext[:, -POOL_STATE:])
    return (xp, xs, jnp.stack(pk), jnp.stack(pv), jnp.stack(pki), jnp.stack(ppool),
            jnp.stack(sk), jnp.stack(sv), jnp.stack(ski), jnp.stack(spool))
```

```python
import functools

import numpy as np
import jax
import jax.numpy as jnp
from jax import lax
from jax.experimental import pallas as pl
from jax.experimental.pallas import tpu as pltpu

F32 = jnp.float32
BF16 = jnp.bfloat16
I32 = jnp.int32

D_MODEL = 2048
PAGE_SIZE = 128
N_HEADS = 8
HEAD_DIM = 128
ATTN_WIDTH = N_HEADS * HEAD_DIM
ROT_DIM = HEAD_DIM // 4
ROPE_THETA = 500000.0
IDX_HEADS = 16
IDX_DIM = 64
IDX_ROT_DIM = IDX_DIM // 4
TOPK_MAX = 256
POOL_WINDOWS = (2, 4, 8, 16)
N_POOL_GROUPS = len(POOL_WINDOWS)
POOL_WIDTH = D_MODEL - ATTN_WIDTH
POOL_GROUP_DIM = POOL_WIDTH // N_POOL_GROUPS
POOL_STATE = max(POOL_WINDOWS) - 1
IN_SPLITS = (ATTN_WIDTH, ATTN_WIDTH, ATTN_WIDTH, ATTN_WIDTH,
             IDX_HEADS * IDX_DIM, IDX_DIM, IDX_HEADS, POOL_WIDTH, POOL_WIDTH)
DEPTH = 1
DEEPNORM_ALPHA = (2 * DEPTH) ** 0.25
LN_EPS = 1e-5
ATTN_SCALE = HEAD_DIM ** -0.5
IDX_SCALE = (IDX_DIM ** -0.5) * (IDX_HEADS ** -0.5)

LANES = 128
VMEM_LIMIT = 56 * 1024 * 1024

_OFF_Q, _OFF_K, _OFF_V, _OFF_GA, _OFF_QI = 0, 1024, 2048, 3072, 4096
_OFF_KI, _OFF_WI, _OFF_U, _OFF_GB = 5120, 5248, 5376, 6400
_PACKED_COLS = 7424

NEG_INF = float("-inf")
M_INIT = -1e30
INT_MIN = -2 ** 31

KEY_CHUNK = 256
Q_TILE = 128


def _dot(a, b):
    return jnp.dot(a, b, preferred_element_type=F32)


def _dot_nt(a, b):
    return lax.dot_general(a, b, (((1,), (1,)), ((), ())), preferred_element_type=F32)


def _sigmoid(x):
    return 1.0 / (1.0 + jnp.exp(-x))


def _pack_w_in(w):
    pts = np.cumsum(IN_SPLITS)[:-1].tolist()
    q, k, v, ga, qi, ki, wi, u, gb = jnp.split(w, pts, axis=-1)
    ki = jnp.pad(ki, ((0, 0), (0, LANES - IDX_DIM)))
    wi = jnp.pad(wi, ((0, 0), (0, LANES - IDX_HEADS)))
    return jnp.concatenate([q, k, v, ga, qi, ki, wi, u, gb], axis=-1).astype(BF16)


def _rope_tables(pos, rot, group):
    half = rot // 2
    freqs = ROPE_THETA ** (-jnp.arange(half, dtype=F32) * (2.0 / rot))
    ang = pos.astype(F32)[:, None] * freqs[None, :]
    cos, sin = jnp.cos(ang), jnp.sin(ang)
    lane = np.arange(LANES) % group
    sel = lane % half
    cos_l, sin_l = cos[:, sel], sin[:, sel]
    first = jnp.asarray(lane < half)[None, :]
    second = jnp.asarray((lane >= half) & (lane < rot))[None, :]
    c = jnp.where(first | second, cos_l, 1.0)
    s1 = jnp.where(first, -sin_l, 0.0)
    s2 = jnp.where(second, sin_l, 0.0)
    return jnp.stack([c, s1, s2]).astype(F32)


def _rope(x, tab_ref, half):
    fwd = pltpu.roll(x, LANES - half, 1)
    bwd = pltpu.roll(x, half, 1)
    return x * tab_ref[0] + fwd * tab_ref[1] + bwd * tab_ref[2]


def _proj_kernel(head_major, x_ref, w_ref, th_ref, ti_ref,
                 q_ref, kf_ref, kb_ref, vf_ref, vb_ref, ga_ref, qi_ref, kif_ref, kib_ref,
                 wi_ref, u_ref, gb_ref):
    xb = x_ref[...].astype(BF16)

    def seg(off, n):
        return _dot(xb, w_ref[:, off:off + n])

    for h in range(N_HEADS):
        sl = slice(h * HEAD_DIM, (h + 1) * HEAD_DIM)
        qh = _rope(seg(_OFF_Q + h * HEAD_DIM, HEAD_DIM), th_ref, ROT_DIM // 2)
        q_ref[:, sl] = (qh * ATTN_SCALE).astype(BF16)
        kh = _rope(seg(_OFF_K + h * HEAD_DIM, HEAD_DIM), th_ref, ROT_DIM // 2)
        kf_ref[:, sl] = kh
        kb_ref[:, sl] = kh.astype(BF16)

    v = seg(_OFF_V, ATTN_WIDTH)
    vf_ref[...] = v
    vb_ref[...] = v.astype(BF16)
    ga_ref[...] = seg(_OFF_GA, ATTN_WIDTH)
    u_ref[...] = seg(_OFF_U, POOL_WIDTH)
    gb_ref[...] = seg(_OFF_GB, POOL_WIDTH)

    for j in range(IDX_HEADS // 2):
        r = _rope(seg(_OFF_QI + j * LANES, LANES), ti_ref, IDX_ROT_DIM // 2).astype(BF16)
        if head_major:
            qi_ref[2 * j] = r[:, :IDX_DIM]
            qi_ref[2 * j + 1] = r[:, IDX_DIM:]
        else:
            qi_ref[:, j * LANES:(j + 1) * LANES] = r

    ki = _rope(seg(_OFF_KI, LANES), ti_ref, IDX_ROT_DIM // 2)[:, :IDX_DIM]
    kif_ref[...] = ki
    kib_ref[...] = ki.astype(BF16)

    wi = seg(_OFF_WI, LANES) * IDX_SCALE
    if head_major:
        wi_ref[...] = wi.T[:IDX_HEADS, :]
    else:
        wi_ref[...] = wi


def _project(x2d, w_p, tab_h, tab_i, *, tm, head_major):
    rows = x2d.shape[0]
    nb = rows // tm
    pos_blocks = tab_h.shape[1] // tm

    row_blk = lambda n: pl.BlockSpec((tm, n), lambda i: (i, 0))
    tab_blk = pl.BlockSpec((3, tm, LANES), lambda i: (0, i % pos_blocks, 0))
    if head_major:
        qi_shape = jax.ShapeDtypeStruct((IDX_HEADS, rows, IDX_DIM), BF16)
        qi_blk = pl.BlockSpec((IDX_HEADS, tm, IDX_DIM), lambda i: (0, i, 0))
        wi_shape = jax.ShapeDtypeStruct((IDX_HEADS, rows), F32)
        wi_blk = pl.BlockSpec((IDX_HEADS, tm), lambda i: (0, i))
    else:
        qi_shape = jax.ShapeDtypeStruct((rows, IDX_HEADS * IDX_DIM), BF16)
        qi_blk = row_blk(IDX_HEADS * IDX_DIM)
        wi_shape = jax.ShapeDtypeStruct((rows, LANES), F32)
        wi_blk = row_blk(LANES)

    wide = lambda dt: jax.ShapeDtypeStruct((rows, ATTN_WIDTH), dt)
    out_shape = (wide(BF16), wide(F32), wide(BF16), wide(F32), wide(BF16), wide(F32),
                 qi_shape,
                 jax.ShapeDtypeStruct((rows, IDX_DIM), F32), jax.ShapeDtypeStruct((rows, IDX_DIM), BF16),
                 wi_shape, wide(F32), wide(F32))
    out_specs = (row_blk(ATTN_WIDTH),) * 6 + (qi_blk, row_blk(IDX_DIM), row_blk(IDX_DIM), wi_blk,
                                              row_blk(POOL_WIDTH), row_blk(POOL_WIDTH))
    return pl.pallas_call(
        functools.partial(_proj_kernel, head_major),
        grid=(nb,),
        in_specs=[row_blk(D_MODEL),
                  pl.BlockSpec((D_MODEL, _PACKED_COLS), lambda i: (0, 0), pipeline_mode=pl.Buffered(1)),
                  tab_blk, tab_blk],
        out_specs=out_specs,
        out_shape=out_shape,
        compiler_params=pltpu.CompilerParams(dimension_semantics=("arbitrary",),
                                             vmem_limit_bytes=VMEM_LIMIT),
        name="proj_hm" if head_major else "proj_rm",
    )(x2d, w_p, tab_h, tab_i)


def _sort_key(x):
    bits = pltpu.bitcast(x, I32)
    return bits ^ ((bits >> 31) & 0x7FFFFFFF)


def _prompt_attn_kernel(topk, q_ref, qi_ref, wit_ref, ki_ref, k_ref, v_ref, a_ref,
                        key_s, bias_s, m_s, l_s, acc_s):
    i = pl.program_id(1)
    n_chunks = (i * Q_TILE + Q_TILE + KEY_CHUNK - 1) // KEY_CHUNK
    qpos = i * Q_TILE + lax.broadcasted_iota(I32, (1, Q_TILE), 1)
    row = lax.broadcasted_iota(I32, (KEY_CHUNK, 1), 0)

    def score_chunk(c, carry):
        kic = ki_ref[c]
        acc = jnp.zeros((KEY_CHUNK, Q_TILE), F32)
        for j in range(IDX_HEADS // 2):
            qpair = qi_ref[2 * j:2 * j + 2].reshape(2 * Q_TILE, IDX_DIM)
            st = _dot_nt(kic, qpair)
            acc = acc + jnp.maximum(st[:, :Q_TILE], 0.0) * wit_ref[2 * j:2 * j + 1, :]
            acc = acc + jnp.maximum(st[:, Q_TILE:], 0.0) * wit_ref[2 * j + 1:2 * j + 2, :]
        causal = (c * KEY_CHUNK + row) <= qpos
        key_s[c] = _sort_key(jnp.where(causal, acc, NEG_INF))
        return carry

    lax.fori_loop(0, n_chunks, score_chunk, 0)

    def count_ge(cand):
        def body(c, cnt):
            return cnt + jnp.sum((key_s[c] >= cand).astype(F32), axis=0, keepdims=True)
        return lax.fori_loop(0, n_chunks, body, jnp.zeros((1, Q_TILE), F32))

    def bit_body(it, prefix):
        bit = jnp.left_shift(jnp.int32(1), 31 - it)
        trial = prefix | bit
        cnt = count_ge(trial ^ INT_MIN)
        return jnp.where(cnt >= topk, trial, prefix)

    thr = lax.fori_loop(0, 32, bit_body, jnp.zeros((1, Q_TILE), I32)) ^ INT_MIN

    def bias_chunk(c, carry):
        sel = (key_s[c] >= thr) & ((c * KEY_CHUNK + row) <= qpos)
        bias_s[c] = jnp.where(sel, 0.0, NEG_INF).astype(F32).T
        return carry

    lax.fori_loop(0, n_chunks, bias_chunk, 0)

    m_s[...] = jnp.full(m_s.shape, M_INIT, F32)
    l_s[...] = jnp.zeros(l_s.shape, F32)
    acc_s[...] = jnp.zeros(acc_s.shape, F32)

    def attn_chunk(c, carry):
        bias = bias_s[c]
        for h in range(N_HEADS):
            sl = slice(h * HEAD_DIM, (h + 1) * HEAD_DIM)
            s = _dot_nt(q_ref[:, sl], k_ref[c, :, sl]) + bias
            m_old = m_s[h]
            m_new = jnp.maximum(m_old, jnp.max(s, axis=1, keepdims=True))
            alpha = jnp.exp(m_old - m_new)
            p = jnp.exp(s - m_new)
            l_s[h] = alpha * l_s[h] + jnp.sum(p, axis=1, keepdims=True)
            acc_s[h] = alpha * acc_s[h] + _dot(p.astype(BF16), v_ref[c, :, sl])
            m_s[h] = m_new
        return carry

    lax.fori_loop(0, n_chunks, attn_chunk, 0)

    for h in range(N_HEADS):
        a_ref[:, h * HEAD_DIM:(h + 1) * HEAD_DIM] = acc_s[h] / l_s[h]


def _prompt_attention(q, qi_hm, wit, ki_b, k_b, v_b, *, batch, seq):
    nq = seq // Q_TILE
    nch = seq // KEY_CHUNK
    topk = min(TOPK_MAX, seq // 4)
    ki3 = ki_b.reshape(batch * nch, KEY_CHUNK, IDX_DIM)
    k3 = k_b.reshape(batch * nch, KEY_CHUNK, ATTN_WIDTH)
    v3 = v_b.reshape(batch * nch, KEY_CHUNK, ATTN_WIDTH)
    resident = lambda n: pl.BlockSpec((nch, KEY_CHUNK, n), lambda b, i: (b, 0, 0),
                                      pipeline_mode=pl.Buffered(1))
    return pl.pallas_call(
        functools.partial(_prompt_attn_kernel, topk),
        grid=(batch, nq),
        in_specs=[pl.BlockSpec((Q_TILE, ATTN_WIDTH), lambda b, i: (b * nq + i, 0)),
                  pl.BlockSpec((IDX_HEADS, Q_TILE, IDX_DIM), lambda b, i: (0, b * nq + i, 0)),
                  pl.BlockSpec((IDX_HEADS, Q_TILE), lambda b, i: (0, b * nq + i)),
                  resident(IDX_DIM), resident(ATTN_WIDTH), resident(ATTN_WIDTH)],
        out_specs=pl.BlockSpec((Q_TILE, ATTN_WIDTH), lambda b, i: (b * nq + i, 0)),
        out_shape=jax.ShapeDtypeStruct((batch * seq, ATTN_WIDTH), F32),
        scratch_shapes=[pltpu.VMEM((nch, KEY_CHUNK, Q_TILE), I32),
                        pltpu.VMEM((nch, Q_TILE, KEY_CHUNK), F32),
                        pltpu.VMEM((N_HEADS, Q_TILE, 1), F32),
                        pltpu.VMEM((N_HEADS, Q_TILE, 1), F32),
                        pltpu.VMEM((N_HEADS, Q_TILE, HEAD_DIM), F32)],
        compiler_params=pltpu.CompilerParams(dimension_semantics=("arbitrary", "arbitrary"),
                                             vmem_limit_bytes=VMEM_LIMIT),
        name="prompt_attn",
    )(q, qi_hm, wit, ki3, k3, v3)


HALO = 16


def _gate_out_norm(a, ga, p, gb, x, wo_ref, g_ref, b_ref):
    mixed = jnp.concatenate([a * (ga * _sigmoid(ga)), p * (gb * _sigmoid(gb))], axis=-1)
    h = DEEPNORM_ALPHA * x + _dot(mixed.astype(BF16), wo_ref[...])
    mu = jnp.mean(h, axis=-1, keepdims=True)
    d = h - mu
    var = jnp.mean(d * d, axis=-1, keepdims=True)
    return d * lax.rsqrt(var + LN_EPS) * g_ref[...] + b_ref[...]


def _combine_kernel(blocks_per_seq, a_ref, ga_ref, gb_ref, u_ref, uh_ref, x_ref,
                    wp_ref, ps_ref, wo_ref, g_ref, b_ref, y_ref, ext_s):
    i = pl.program_id(0)
    tm = u_ref.shape[0]
    first = (i % blocks_per_seq) == 0
    ext_s[0:HALO, :] = jnp.where(first, 0.0, uh_ref[...])
    ext_s[HALO:HALO + tm, :] = u_ref[...]
    pos = (i % blocks_per_seq) * tm + lax.broadcasted_iota(I32, (tm, 1), 0)

    parts = []
    for g, w in enumerate(POOL_WINDOWS):
        cols = slice(g * POOL_GROUP_DIM, (g + 1) * POOL_GROUP_DIM)
        win = ext_s[HALO:HALO + tm, cols]
        for j in range(1, w):
            win = win + ext_s[HALO - j:HALO - j + tm, cols]
        count = jnp.minimum(pos + 1, w).astype(F32)
        pooled = win / count - u_ref[:, cols]
        parts.append(_dot(pooled.astype(BF16), wp_ref[g]))
    p = jnp.concatenate(parts, axis=-1) * ps_ref[...]
    y_ref[...] = _gate_out_norm(a_ref[...], ga_ref[...], p, gb_ref[...], x_ref[...],
                                wo_ref, g_ref, b_ref)


def _combine(a, ga, gb, u, x2d, w_pool_b, pool_scale, w_out_b, ln_g, ln_b, *, seq, tm):
    rows = a.shape[0]
    nb = rows // tm
    bps = seq // tm
    hpb = tm // HALO
    row_blk = lambda n: pl.BlockSpec((tm, n), lambda i: (i, 0))
    const = lambda shape: pl.BlockSpec(shape, lambda i: (0,) * len(shape))
    return pl.pallas_call(
        functools.partial(_combine_kernel, bps),
        grid=(nb,),
        in_specs=[row_blk(ATTN_WIDTH), row_blk(ATTN_WIDTH), row_blk(POOL_WIDTH), row_blk(POOL_WIDTH),
                  pl.BlockSpec((HALO, POOL_WIDTH), lambda i: (jnp.maximum(i * hpb - 1, 0), 0)),
                  row_blk(D_MODEL),
                  const((N_POOL_GROUPS, POOL_GROUP_DIM, POOL_GROUP_DIM)), const((1, POOL_WIDTH)),
                  const((D_MODEL, D_MODEL)), const((1, D_MODEL)), const((1, D_MODEL))],
        out_specs=row_blk(D_MODEL),
        out_shape=jax.ShapeDtypeStruct((rows, D_MODEL), F32),
        scratch_shapes=[pltpu.VMEM((HALO + tm, POOL_WIDTH), F32)],
        compiler_params=pltpu.CompilerParams(dimension_semantics=("arbitrary",),
                                             vmem_limit_bytes=VMEM_LIMIT),
        name="combine",
    )(a, ga, gb, u, u, x2d, w_pool_b, pool_scale, w_out_b, ln_g, ln_b)


def _sample_scores_kernel(n_pages, t_new, pt_ref, qi_ref, w_ref, kn_ref, *refs):
    page_refs, sc_ref = refs[:n_pages], refs[n_pages]
    qi = qi_ref[0]
    w = w_ref[0]

    def head_sum(s):
        r = jnp.maximum(s, 0.0) * w
        return jnp.sum(r.reshape(t_new, IDX_HEADS, s.shape[-1]), axis=1)

    for p in range(n_pages):
        kp = page_refs[p][0].astype(BF16)
        sc_ref[0, :, p * PAGE_SIZE:(p + 1) * PAGE_SIZE] = head_sum(_dot_nt(qi, kp))
    s_new = head_sum(_dot_nt(qi, kn_ref[0]))
    t_idx = lax.broadcasted_iota(I32, s_new.shape, 0)
    j_idx = lax.broadcasted_iota(I32, s_new.shape, 1)
    sc_ref[0, :, n_pages * PAGE_SIZE:] = jnp.where(j_idx <= t_idx, s_new, NEG_INF)


def _sample_scores(page_table, qi_rows, w_col, ki_new_pad, cache_kidx2, *, t_new):
    nb, n_pages = page_table.shape
    width = (n_pages + 1) * PAGE_SIZE
    page_spec = lambda p: pl.BlockSpec((1, PAGE_SIZE, IDX_DIM), lambda b, pt, p=p: (pt[b, p], 0, 0))
    grid_spec = pltpu.PrefetchScalarGridSpec(
        num_scalar_prefetch=1,
        grid=(nb,),
        in_specs=[pl.BlockSpec((1, t_new * IDX_HEADS, IDX_DIM), lambda b, pt: (b, 0, 0)),
                  pl.BlockSpec((1, t_new * IDX_HEADS, 1), lambda b, pt: (b, 0, 0)),
                  pl.BlockSpec((1, PAGE_SIZE, IDX_DIM), lambda b, pt: (b, 0, 0))]
                 + [page_spec(p) for p in range(n_pages)],
        out_specs=pl.BlockSpec((1, t_new, width), lambda b, pt: (b, 0, 0)),
    )
    return pl.pallas_call(
        functools.partial(_sample_scores_kernel, n_pages, t_new),
        grid_spec=grid_spec,
        out_shape=jax.ShapeDtypeStruct((nb, t_new, width), F32),
        compiler_params=pltpu.CompilerParams(dimension_semantics=("arbitrary",),
                                             vmem_limit_bytes=VMEM_LIMIT),
        name="sample_scores",
    )(page_table, qi_rows, w_col, ki_new_pad, *([cache_kidx2] * n_pages))


def _sample_select_kernel(topk, sc_ref, bias_ref, key_s):
    key_s[...] = _sort_key(sc_ref[...])
    rows = sc_ref.shape[0]

    def bit_body(it, prefix):
        bit = jnp.left_shift(jnp.int32(1), 31 - it)
        trial = prefix | bit
        ge = (key_s[...] >= (trial ^ INT_MIN)).astype(F32)
        cnt = jnp.sum(ge, axis=1, keepdims=True)
        return jnp.where(cnt >= topk, trial, prefix)

    thr = lax.fori_loop(0, 32, bit_body, jnp.zeros((rows, 1), I32)) ^ INT_MIN
    sel = (key_s[...] >= thr) & (sc_ref[...] > NEG_INF)
    bias_ref[...] = jnp.where(sel, 0.0, NEG_INF).astype(F32)


def _sample_select(scores2d, *, topk):
    rows, width = scores2d.shape
    return pl.pallas_call(
        functools.partial(_sample_select_kernel, topk),
        out_shape=jax.ShapeDtypeStruct((rows, width), F32),
        scratch_shapes=[pltpu.VMEM((rows, width), I32)],
        compiler_params=pltpu.CompilerParams(vmem_limit_bytes=VMEM_LIMIT),
        name="sample_select",
    )(scores2d)


Q_PAD = 8


def _sample_attn_kernel(n_pages, pt_ref, q_ref, kp_ref, vp_ref, bp_ref, kn_ref, vn_ref, bn_ref,
                        a_ref, m_s, l_s, acc_s):
    p = pl.program_id(1)

    @pl.when(p == 0)
    def _():
        m_s[...] = jnp.full(m_s.shape, M_INIT, F32)
        l_s[...] = jnp.zeros(l_s.shape, F32)
        acc_s[...] = jnp.zeros(acc_s.shape, F32)

    def attend(k, v, bias):
        for h in range(N_HEADS):
            sl = slice(h * HEAD_DIM, (h + 1) * HEAD_DIM)
            s = _dot_nt(q_ref[0, :, sl], k[:, sl]) + bias
            m_old = m_s[h]
            m_new = jnp.maximum(m_old, jnp.max(s, axis=1, keepdims=True))
            alpha = jnp.exp(m_old - m_new)
            pr = jnp.exp(s - m_new)
            l_s[h] = alpha * l_s[h] + jnp.sum(pr, axis=1, keepdims=True)
            acc_s[h] = alpha * acc_s[h] + _dot(pr.astype(BF16), v[:, sl])
            m_s[h] = m_new

    attend(kp_ref[0].astype(BF16), vp_ref[0].astype(BF16), bp_ref[0])

    @pl.when(p == n_pages - 1)
    def _():
        attend(kn_ref[0], vn_ref[0], bn_ref[0])
        for h in range(N_HEADS):
            a_ref[0, :, h * HEAD_DIM:(h + 1) * HEAD_DIM] = acc_s[h] / l_s[h]


def _sample_attention(page_table, q_pad, bias_pad, k_new_pad, v_new_pad, cache_k3, cache_v3):
    nb, n_pages = page_table.shape
    page = lambda: pl.BlockSpec((1, PAGE_SIZE, ATTN_WIDTH), lambda b, p, pt: (pt[b, p], 0, 0))
    per_b = lambda r, n: pl.BlockSpec((1, r, n), lambda b, p, pt: (b, 0, 0))
    grid_spec = pltpu.PrefetchScalarGridSpec(
        num_scalar_prefetch=1,
        grid=(nb, n_pages),
        in_specs=[per_b(Q_PAD, ATTN_WIDTH), page(), page(),
                  pl.BlockSpec((1, Q_PAD, PAGE_SIZE), lambda b, p, pt: (b, 0, p)),
                  per_b(PAGE_SIZE, ATTN_WIDTH), per_b(PAGE_SIZE, ATTN_WIDTH),
                  pl.BlockSpec((1, Q_PAD, PAGE_SIZE), lambda b, p, pt: (b, 0, n_pages))],
        out_specs=per_b(Q_PAD, ATTN_WIDTH),
        scratch_shapes=[pltpu.VMEM((N_HEADS, Q_PAD, 1), F32),
                        pltpu.VMEM((N_HEADS, Q_PAD, 1), F32),
                        pltpu.VMEM((N_HEADS, Q_PAD, HEAD_DIM), F32)],
    )
    return pl.pallas_call(
        functools.partial(_sample_attn_kernel, n_pages),
        grid_spec=grid_spec,
        out_shape=jax.ShapeDtypeStruct((nb, Q_PAD, ATTN_WIDTH), F32),
        compiler_params=pltpu.CompilerParams(dimension_semantics=("arbitrary", "arbitrary"),
                                             vmem_limit_bytes=VMEM_LIMIT),
        name="sample_attn",
    )(page_table, q_pad, cache_k3, cache_v3, bias_pad, k_new_pad, v_new_pad, bias_pad)


def _sample_combine_kernel(t_new, a_ref, ga_ref, gb_ref, ext_ref, x_ref,
                           wp_ref, ps_ref, wo_ref, g_ref, b_ref, y_ref):
    p_t = []
    for t in range(t_new):
        parts = []
        for g, w in enumerate(POOL_WINDOWS):
            cols = slice(g * POOL_GROUP_DIM, (g + 1) * POOL_GROUP_DIM)
            cur = POOL_STATE + t
            win = ext_ref[cur, :, cols]
            for j in range(1, w):
                win = win + ext_ref[cur - j, :, cols]
            pooled = win / float(w) - ext_ref[cur, :, cols]
            parts.append(_dot(pooled.astype(BF16), wp_ref[g]))
        p_t.append(jnp.concatenate(parts, axis=-1) * ps_ref[...])
    y_ref[...] = _gate_out_norm(a_ref[...], ga_ref[...], jnp.concatenate(p_t, axis=0), gb_ref[...],
                                x_ref[...], wo_ref, g_ref, b_ref)


def _sample_combine(a, ga, gb, ext_tm, x2d, w_pool_b, pool_scale, w_out_b, ln_g, ln_b, *, t_new):
    return pl.pallas_call(
        functools.partial(_sample_combine_kernel, t_new),
        out_shape=jax.ShapeDtypeStruct(x2d.shape, F32),
        compiler_params=pltpu.CompilerParams(vmem_limit_bytes=VMEM_LIMIT),
        name="sample_combine",
    )(a, ga, gb, ext_tm, x2d, w_pool_b, pool_scale, w_out_b, ln_g, ln_b)


def kernel(x_prompt, x_sample, cache_k, cache_v, cache_kidx, state_pool, page_table,
           w_in, w_pool, pool_scale, w_out, ln_g, ln_b):
    assert w_in.shape[0] == DEPTH == 1
    batch, seq, _ = x_prompt.shape
    nb, t_new, _ = x_sample.shape
    n_pages = page_table.shape[1]
    past = n_pages * PAGE_SIZE

    w_p = _pack_w_in(w_in[0])
    w_pool_b = w_pool[0].astype(BF16)
    w_out_b = w_out[0].astype(BF16)
    ps, g, b = pool_scale[0][None, :], ln_g[0][None, :], ln_b[0][None, :]

    pos_p = jnp.arange(seq)
    xp2d = x_prompt.reshape(batch * seq, D_MODEL)
    (q, kf, kb, vf, vb, ga, qi_hm, kif, kib, wit, u, gb) = _project(
        xp2d, w_p, _rope_tables(pos_p, ROT_DIM, HEAD_DIM), _rope_tables(pos_p, IDX_ROT_DIM, IDX_DIM),
        tm=256, head_major=True)
    a = _prompt_attention(q, qi_hm, wit, kib, kb, vb, batch=batch, seq=seq)
    y_prompt = _combine(a, ga, gb, u, xp2d, w_pool_b, ps, w_out_b, g, b, seq=seq, tm=256)

    rows_s = nb * t_new
    pos_s = jnp.tile(past + jnp.arange(t_new), nb)
    xs2d = x_sample.reshape(rows_s, D_MODEL)
    (qs, kfs, kbs, vfs, vbs, gas, qis, kifs, kibs, wis, us, gbs) = _project(
        xs2d, w_p, _rope_tables(pos_s, ROT_DIM, HEAD_DIM), _rope_tables(pos_s, IDX_ROT_DIM, IDX_DIM),
        tm=rows_s, head_major=False)

    pad_rows = lambda z, n: jnp.pad(z.reshape(nb, t_new, z.shape[-1]), ((0, 0), (0, n - t_new), (0, 0)))
    scores = _sample_scores(
        page_table,
        qis.reshape(nb, t_new * IDX_HEADS, IDX_DIM),
        wis[:, :IDX_HEADS].reshape(nb, t_new * IDX_HEADS, 1),
        pad_rows(kibs, PAGE_SIZE),
        cache_kidx.reshape(cache_kidx.shape[1], PAGE_SIZE, IDX_DIM),
        t_new=t_new)
    width = scores.shape[-1]
    bias = _sample_select(scores.reshape(rows_s, width), topk=min(TOPK_MAX, (past + t_new) // 4))
    a_s = _sample_attention(
        page_table, pad_rows(qs, Q_PAD),
        jnp.pad(bias.reshape(nb, t_new, width), ((0, 0), (0, Q_PAD - t_new), (0, 0))),
        pad_rows(kbs, PAGE_SIZE), pad_rows(vbs, PAGE_SIZE),
        cache_k.reshape(cache_k.shape[1], PAGE_SIZE, ATTN_WIDTH),
        cache_v.reshape(cache_v.shape[1], PAGE_SIZE, ATTN_WIDTH))
    a_s = a_s[:, :t_new].reshape(rows_s, ATTN_WIDTH)

    u_ext = jnp.concatenate([state_pool[0], us.reshape(nb, t_new, POOL_WIDTH)], axis=1)
    time_major = lambda z: z.reshape(nb, t_new, z.shape[-1]).transpose(1, 0, 2).reshape(rows_s, z.shape[-1])
    y_tm = _sample_combine(time_major(a_s), time_major(gas), time_major(gbs), u_ext.transpose(1, 0, 2),
                           time_major(xs2d), w_pool_b, ps, w_out_b, g, b, t_new=t_new)
    y_sample = y_tm.reshape(t_new, nb, D_MODEL).transpose(1, 0, 2)

    hd = (N_HEADS, HEAD_DIM)
    return (y_prompt.reshape(batch, seq, D_MODEL),
            y_sample.reshape(nb, t_new, D_MODEL),
            kf.reshape(1, batch, seq, *hd), vf.reshape(1, batch, seq, *hd),
            kif.reshape(1, batch, seq, IDX_DIM),
            u.reshape(batch, seq, POOL_WIDTH)[None, :, -POOL_STATE:],
            kfs.reshape(1, nb, t_new, *hd), vfs.reshape(1, nb, t_new, *hd),
            kifs.reshape(1, nb, t_new, IDX_DIM),
            u_ext[None, :, -POOL_STATE:])
```

```python
import functools

import numpy as np
import jax
import jax.numpy as jnp
from jax import lax
from jax.experimental import pallas as pl
from jax.experimental.pallas import tpu as pltpu

F32 = jnp.float32
BF16 = jnp.bfloat16
I32 = jnp.int32

D_MODEL = 2048
PAGE_SIZE = 128
N_HEADS = 8
HEAD_DIM = 128
ATTN_WIDTH = N_HEADS * HEAD_DIM
ROT_DIM = HEAD_DIM // 4
ROPE_THETA = 500000.0
IDX_HEADS = 16
IDX_DIM = 64
IDX_ROT_DIM = IDX_DIM // 4
TOPK_MAX = 256
POOL_WINDOWS = (2, 4, 8, 16)
N_POOL_GROUPS = len(POOL_WINDOWS)
POOL_WIDTH = D_MODEL - ATTN_WIDTH
POOL_GROUP_DIM = POOL_WIDTH // N_POOL_GROUPS
POOL_STATE = max(POOL_WINDOWS) - 1
IN_SPLITS = (ATTN_WIDTH, ATTN_WIDTH, ATTN_WIDTH, ATTN_WIDTH,
             IDX_HEADS * IDX_DIM, IDX_DIM, IDX_HEADS, POOL_WIDTH, POOL_WIDTH)
IN_COLS = sum(IN_SPLITS)
DEPTH = 1
DEEPNORM_ALPHA = (2 * DEPTH) ** 0.25
LN_EPS = 1e-5
ATTN_SCALE = HEAD_DIM ** -0.5
IDX_SCALE = (IDX_DIM ** -0.5) * (IDX_HEADS ** -0.5)

LANES = 128
SUBLANES = 8
VMEM_LIMIT = 56 * 1024 * 1024

(_OFF_Q, _OFF_K, _OFF_V, _OFF_GA, _OFF_QI, _OFF_KI, _OFF_WI, _OFF_U, _OFF_GB) = (
    [0] + np.cumsum(IN_SPLITS)[:-1].tolist())

NEG_INF = float("-inf")
M_INIT = -1e30
INT_MIN = -2 ** 31
KEY_OF_NEG_INF = -2139095041

KEY_CHUNK = 256
Q_TILE = 128
PAGES_PER_STEP = 8


def _dot(a, b):
    return jnp.dot(a, b, preferred_element_type=F32)


def _dot_nt(a, b):
    return lax.dot_general(a, b, (((1,), (1,)), ((), ())), preferred_element_type=F32)


def _sigmoid(x):
    return 1.0 / (1.0 + jnp.exp(-x))


def _rope_tables_t(pos, rot):
    half = rot // 2
    freqs = ROPE_THETA ** (-jnp.arange(half, dtype=F32) * (2.0 / rot))
    ang = pos.astype(F32)[:, None] * freqs[None, :]
    return jnp.stack([jnp.cos(ang).T, jnp.sin(ang).T]).astype(F32)


def _rope_tables(pos, rot, group):
    half = rot // 2
    freqs = ROPE_THETA ** (-jnp.arange(half, dtype=F32) * (2.0 / rot))
    ang = pos.astype(F32)[:, None] * freqs[None, :]
    cos, sin = jnp.cos(ang), jnp.sin(ang)
    lane = np.arange(LANES) % group
    sel = lane % half
    cos_l, sin_l = cos[:, sel], sin[:, sel]
    first = jnp.asarray(lane < half)[None, :]
    second = jnp.asarray((lane >= half) & (lane < rot))[None, :]
    c = jnp.where(first | second, cos_l, 1.0)
    s1 = jnp.where(first, -sin_l, 0.0)
    s2 = jnp.where(second, sin_l, 0.0)
    return jnp.stack([c, s1, s2]).astype(F32)


def _rope(x, tab_ref, half):
    fwd = pltpu.roll(x, LANES - half, 1)
    bwd = pltpu.roll(x, half, 1)
    return x * tab_ref[0] + fwd * tab_ref[1] + bwd * tab_ref[2]


def _proj_kernel(head_major, x_ref, wt_ref, th_ref, ti_ref, tit_ref,
                 q_ref, kf_ref, kb_ref, vf_ref, vb_ref, ga_ref, qi_ref, kif_ref, kib_ref, kit_ref,
                 wit_ref, u_ref, gb_ref):
    xb = x_ref[...].astype(BF16)

    def seg(off, n):
        return _dot_nt(xb, wt_ref[off:off + n, :])

    def seg_t(off, n):
        return _dot_nt(wt_ref[off:off + n, :], xb)

    for h in range(N_HEADS):
        sl = slice(h * HEAD_DIM, (h + 1) * HEAD_DIM)
        qh = _rope(seg(_OFF_Q + h * HEAD_DIM, HEAD_DIM), th_ref, ROT_DIM // 2)
        q_ref[:, sl] = (qh * ATTN_SCALE).astype(BF16)
        kh = _rope(seg(_OFF_K + h * HEAD_DIM, HEAD_DIM), th_ref, ROT_DIM // 2)
        kf_ref[:, sl] = kh
        kb_ref[:, sl] = kh.astype(BF16)

    v = seg(_OFF_V, ATTN_WIDTH)
    vf_ref[...] = v
    vb_ref[...] = v.astype(BF16)
    ga_ref[...] = seg(_OFF_GA, ATTN_WIDTH)
    u_ref[...] = seg(_OFF_U, POOL_WIDTH)
    gb_ref[...] = seg(_OFF_GB, POOL_WIDTH)

    for j in range(IDX_HEADS // 2):
        r = _rope(seg(_OFF_QI + j * LANES, LANES), ti_ref, IDX_ROT_DIM // 2).astype(BF16)
        if head_major:
            qi_ref[2 * j] = r[:, :IDX_DIM]
            qi_ref[2 * j + 1] = r[:, IDX_DIM:]
        else:
            qi_ref[:, j * LANES:(j + 1) * LANES] = r

    half = IDX_ROT_DIM // 2
    kt = seg_t(_OFF_KI, IDX_DIM)
    x1, x2 = kt[:half], kt[half:2 * half]
    cos_t, sin_t = tit_ref[0], tit_ref[1]
    kit_ref[0] = jnp.concatenate([x1 * cos_t - x2 * sin_t, x1 * sin_t + x2 * cos_t, kt[2 * half:]], axis=0)
    kn = jnp.concatenate([seg(_OFF_KI, IDX_DIM), jnp.zeros((xb.shape[0], LANES - IDX_DIM), F32)], axis=1)
    ki = _rope(kn, ti_ref, half)[:, :IDX_DIM]
    kif_ref[...] = ki
    kib_ref[...] = ki.astype(BF16)

    wit_ref[...] = seg_t(_OFF_WI, IDX_HEADS) * IDX_SCALE


def _project(x2d, w_t, pos, *, tm, rows_per_seq, head_major):
    rows = x2d.shape[0]
    nb = rows // tm
    pos_blocks = pos.shape[0] // tm
    seq_blocks = rows_per_seq // tm
    tab_h = _rope_tables(pos, ROT_DIM, HEAD_DIM)
    tab_i = _rope_tables(pos, IDX_ROT_DIM, IDX_DIM)
    tab_it = _rope_tables_t(pos, IDX_ROT_DIM)

    row_blk = lambda n: pl.BlockSpec((tm, n), lambda i: (i, 0))
    tab_blk = pl.BlockSpec((3, tm, LANES), lambda i: (0, i % pos_blocks, 0))
    tabt_blk = pl.BlockSpec((2, IDX_ROT_DIM // 2, tm), lambda i: (0, 0, i % pos_blocks))
    if head_major:
        qi_shape = jax.ShapeDtypeStruct((IDX_HEADS, rows, IDX_DIM), BF16)
        qi_blk = pl.BlockSpec((IDX_HEADS, tm, IDX_DIM), lambda i: (0, i, 0))
    else:
        qi_shape = jax.ShapeDtypeStruct((rows, IDX_HEADS * IDX_DIM), BF16)
        qi_blk = row_blk(IDX_HEADS * IDX_DIM)

    wide = lambda dt: jax.ShapeDtypeStruct((rows, ATTN_WIDTH), dt)
    out_shape = (wide(BF16), wide(F32), wide(BF16), wide(F32), wide(BF16), wide(F32),
                 qi_shape,
                 jax.ShapeDtypeStruct((rows, IDX_DIM), F32), jax.ShapeDtypeStruct((rows, IDX_DIM), BF16),
                 jax.ShapeDtypeStruct((rows // rows_per_seq, IDX_DIM, rows_per_seq), F32),
                 jax.ShapeDtypeStruct((IDX_HEADS, rows), F32), wide(F32), wide(F32))
    out_specs = (row_blk(ATTN_WIDTH),) * 6 + (
        qi_blk, row_blk(IDX_DIM), row_blk(IDX_DIM),
        pl.BlockSpec((1, IDX_DIM, tm), lambda i: (i // seq_blocks, 0, i % seq_blocks)),
        pl.BlockSpec((IDX_HEADS, tm), lambda i: (0, i)),
        row_blk(POOL_WIDTH), row_blk(POOL_WIDTH))
    return pl.pallas_call(
        functools.partial(_proj_kernel, head_major),
        grid=(nb,),
        in_specs=[row_blk(D_MODEL),
                  pl.BlockSpec((IN_COLS, D_MODEL), lambda i: (0, 0), pipeline_mode=pl.Buffered(1)),
                  tab_blk, tab_blk, tabt_blk],
        out_specs=out_specs,
        out_shape=out_shape,
        compiler_params=pltpu.CompilerParams(dimension_semantics=("arbitrary",),
                                             vmem_limit_bytes=VMEM_LIMIT),
        name="proj_hm" if head_major else "proj_rm",
    )(x2d, w_t, tab_h, tab_i, tab_it)


def _key_to_float(key):
    return pltpu.bitcast(key ^ ((key >> 31) & 0x7FFFFFFF), F32)


def _kth_largest(count_ge, topk, shape):
    def bit_body(it, prefix):
        trial = prefix | jnp.left_shift(jnp.int32(1), 31 - it)
        cnt = count_ge(_key_to_float(trial ^ INT_MIN))
        return jnp.where(cnt >= topk, trial, prefix)

    prefix = lax.fori_loop(0, 32, bit_body, jnp.zeros(shape, I32))
    return _key_to_float(jnp.maximum(prefix ^ INT_MIN, KEY_OF_NEG_INF))


def _prompt_attn_kernel(topk, q_ref, qi_ref, wit_ref, ki_ref, k_ref, v_ref, a_ref,
                        sc_s, bias_s, m_s, l_s, acc_s):
    i = pl.program_id(1)
    n_chunks = (i * Q_TILE + Q_TILE + KEY_CHUNK - 1) // KEY_CHUNK
    qpos = i * Q_TILE + lax.broadcasted_iota(I32, (1, Q_TILE), 1)
    row = lax.broadcasted_iota(I32, (KEY_CHUNK, 1), 0)

    def score_chunk(c, carry):
        kic = ki_ref[c]
        acc = jnp.zeros((KEY_CHUNK, Q_TILE), F32)
        for j in range(IDX_HEADS // 2):
            qpair = qi_ref[2 * j:2 * j + 2].reshape(2 * Q_TILE, IDX_DIM)
            st = _dot_nt(kic, qpair)
            acc = acc + jnp.maximum(st[:, :Q_TILE], 0.0) * wit_ref[2 * j:2 * j + 1, :]
            acc = acc + jnp.maximum(st[:, Q_TILE:], 0.0) * wit_ref[2 * j + 1:2 * j + 2, :]
        causal = (c * KEY_CHUNK + row) <= qpos
        sc_s[c] = jnp.where(causal, acc, NEG_INF)
        return carry

    lax.fori_loop(0, n_chunks, score_chunk, 0)

    def count_ge(cand):
        def body(c, cnt):
            return cnt + jnp.sum((sc_s[c] >= cand).astype(F32), axis=0, keepdims=True)
        return lax.fori_loop(0, n_chunks, body, jnp.zeros((1, Q_TILE), F32))

    thr = _kth_largest(count_ge, topk, (1, Q_TILE))

    def bias_chunk(c, carry):
        sel = (sc_s[c] >= thr) & ((c * KEY_CHUNK + row) <= qpos)
        bias_s[c] = jnp.where(sel, 0.0, NEG_INF).astype(F32).T
        return carry

    lax.fori_loop(0, n_chunks, bias_chunk, 0)

    m_s[...] = jnp.full(m_s.shape, M_INIT, F32)
    l_s[...] = jnp.zeros(l_s.shape, F32)
    acc_s[...] = jnp.zeros(acc_s.shape, F32)

    def attn_chunk(c, carry):
        bias = bias_s[c]
        for h in range(N_HEADS):
            sl = slice(h * HEAD_DIM, (h + 1) * HEAD_DIM)
            s = _dot_nt(q_ref[:, sl], k_ref[c, :, sl]) + bias
            m_old = m_s[h]
            m_new = jnp.maximum(m_old, jnp.max(s, axis=1, keepdims=True))
            alpha = jnp.exp(m_old - m_new)
            p = jnp.exp(s - m_new)
            l_s[h] = alpha * l_s[h] + jnp.sum(p, axis=1, keepdims=True)
            acc_s[h] = alpha * acc_s[h] + _dot(p.astype(BF16), v_ref[c, :, sl])
            m_s[h] = m_new
        return carry

    lax.fori_loop(0, n_chunks, attn_chunk, 0)

    for h in range(N_HEADS):
        a_ref[:, h * HEAD_DIM:(h + 1) * HEAD_DIM] = acc_s[h] / l_s[h]


def _prompt_attention(q, qi_hm, wit, ki_b, k_b, v_b, *, batch, seq):
    nq = seq // Q_TILE
    nch = seq // KEY_CHUNK
    topk = min(TOPK_MAX, seq // 4)
    ki3 = ki_b.reshape(batch * nch, KEY_CHUNK, IDX_DIM)
    k3 = k_b.reshape(batch * nch, KEY_CHUNK, ATTN_WIDTH)
    v3 = v_b.reshape(batch * nch, KEY_CHUNK, ATTN_WIDTH)
    resident = lambda n: pl.BlockSpec((nch, KEY_CHUNK, n), lambda b, i: (b, 0, 0),
                                      pipeline_mode=pl.Buffered(1))
    return pl.pallas_call(
        functools.partial(_prompt_attn_kernel, topk),
        grid=(batch, nq),
        in_specs=[pl.BlockSpec((Q_TILE, ATTN_WIDTH), lambda b, i: (b * nq + i, 0)),
                  pl.BlockSpec((IDX_HEADS, Q_TILE, IDX_DIM), lambda b, i: (0, b * nq + i, 0)),
                  pl.BlockSpec((IDX_HEADS, Q_TILE), lambda b, i: (0, b * nq + i)),
                  resident(IDX_DIM), resident(ATTN_WIDTH), resident(ATTN_WIDTH)],
        out_specs=pl.BlockSpec((Q_TILE, ATTN_WIDTH), lambda b, i: (b * nq + i, 0)),
        out_shape=jax.ShapeDtypeStruct((batch * seq, ATTN_WIDTH), F32),
        scratch_shapes=[pltpu.VMEM((nch, KEY_CHUNK, Q_TILE), F32),
                        pltpu.VMEM((nch, Q_TILE, KEY_CHUNK), F32),
                        pltpu.VMEM((N_HEADS, Q_TILE, 1), F32),
                        pltpu.VMEM((N_HEADS, Q_TILE, 1), F32),
                        pltpu.VMEM((N_HEADS, Q_TILE, HEAD_DIM), F32)],
        compiler_params=pltpu.CompilerParams(dimension_semantics=("arbitrary", "arbitrary"),
                                             vmem_limit_bytes=VMEM_LIMIT),
        name="prompt_attn",
    )(q, qi_hm, wit, ki3, k3, v3)


HALO = 16


def _gate_out_norm(a, ga, p, gb, x, wo_ref, g_ref, b_ref):
    mixed = jnp.concatenate([a * (ga * _sigmoid(ga)), p * (gb * _sigmoid(gb))], axis=-1)
    h = DEEPNORM_ALPHA * x + _dot(mixed.astype(BF16), wo_ref[...])
    mu = jnp.mean(h, axis=-1, keepdims=True)
    d = h - mu
    var = jnp.mean(d * d, axis=-1, keepdims=True)
    return d * lax.rsqrt(var + LN_EPS) * g_ref[...] + b_ref[...]


def _combine_kernel(blocks_per_seq, a_ref, ga_ref, gb_ref, u_ref, uh_ref, x_ref,
                    wp_ref, ps_ref, wo_ref, g_ref, b_ref, y_ref, ext_s):
    i = pl.program_id(0)
    tm = u_ref.shape[0]
    first = (i % blocks_per_seq) == 0
    ext_s[0:HALO, :] = jnp.where(first, 0.0, uh_ref[...])
    ext_s[HALO:HALO + tm, :] = u_ref[...]
    pos = (i % blocks_per_seq) * tm + lax.broadcasted_iota(I32, (tm, 1), 0)

    parts = []
    for g, w in enumerate(POOL_WINDOWS):
        cols = slice(g * POOL_GROUP_DIM, (g + 1) * POOL_GROUP_DIM)
        win = ext_s[HALO:HALO + tm, cols]
        for j in range(1, w):
            win = win + ext_s[HALO - j:HALO - j + tm, cols]
        count = jnp.minimum(pos + 1, w).astype(F32)
        pooled = win / count - u_ref[:, cols]
        parts.append(_dot(pooled.astype(BF16), wp_ref[g]))
    p = jnp.concatenate(parts, axis=-1) * ps_ref[...]
    y_ref[...] = _gate_out_norm(a_ref[...], ga_ref[...], p, gb_ref[...], x_ref[...],
                                wo_ref, g_ref, b_ref)


def _combine(a, ga, gb, u, x2d, w_pool_b, pool_scale, w_out_b, ln_g, ln_b, *, seq, tm):
    rows = a.shape[0]
    nb = rows // tm
    bps = seq // tm
    hpb = tm // HALO
    row_blk = lambda n: pl.BlockSpec((tm, n), lambda i: (i, 0))
    const = lambda shape: pl.BlockSpec(shape, lambda i: (0,) * len(shape))
    return pl.pallas_call(
        functools.partial(_combine_kernel, bps),
        grid=(nb,),
        in_specs=[row_blk(ATTN_WIDTH), row_blk(ATTN_WIDTH), row_blk(POOL_WIDTH), row_blk(POOL_WIDTH),
                  pl.BlockSpec((HALO, POOL_WIDTH), lambda i: (jnp.maximum(i * hpb - 1, 0), 0)),
                  row_blk(D_MODEL),
                  const((N_POOL_GROUPS, POOL_GROUP_DIM, POOL_GROUP_DIM)), const((1, POOL_WIDTH)),
                  const((D_MODEL, D_MODEL)), const((1, D_MODEL)), const((1, D_MODEL))],
        out_specs=row_blk(D_MODEL),
        out_shape=jax.ShapeDtypeStruct((rows, D_MODEL), F32),
        scratch_shapes=[pltpu.VMEM((HALO + tm, POOL_WIDTH), F32)],
        compiler_params=pltpu.CompilerParams(dimension_semantics=("arbitrary",),
                                             vmem_limit_bytes=VMEM_LIMIT),
        name="combine",
    )(a, ga, gb, u, u, x2d, w_pool_b, pool_scale, w_out_b, ln_g, ln_b)


def _sample_scores_kernel(n_pages, t_new, pt_ref, qi_ref, w_ref, kn_ref, *refs):
    page_refs, sc_ref = refs[:n_pages], refs[n_pages]
    qi = qi_ref[0]
    w = w_ref[0]

    def head_sum(s):
        r = jnp.maximum(s, 0.0) * w
        return jnp.sum(r.reshape(t_new, IDX_HEADS, s.shape[-1]), axis=1)

    for p in range(n_pages):
        kp_t = page_refs[p][0].astype(BF16)
        sc_ref[0, :, p * PAGE_SIZE:(p + 1) * PAGE_SIZE] = head_sum(_dot(qi, kp_t))
    s_new = head_sum(_dot_nt(qi, kn_ref[0]))
    t_idx = lax.broadcasted_iota(I32, s_new.shape, 0)
    j_idx = lax.broadcasted_iota(I32, s_new.shape, 1)
    sc_ref[0, :, n_pages * PAGE_SIZE:] = jnp.where(j_idx <= t_idx, s_new, NEG_INF)


def _sample_scores(page_table, qi_rows, w_col, ki_new_pad, cache_kidx_t, *, t_new):
    nb, n_pages = page_table.shape
    width = (n_pages + 1) * PAGE_SIZE
    page_spec = lambda p: pl.BlockSpec((1, IDX_DIM, PAGE_SIZE), lambda b, pt, p=p: (pt[b, p], 0, 0))
    grid_spec = pltpu.PrefetchScalarGridSpec(
        num_scalar_prefetch=1,
        grid=(nb,),
        in_specs=[pl.BlockSpec((1, t_new * IDX_HEADS, IDX_DIM), lambda b, pt: (b, 0, 0)),
                  pl.BlockSpec((1, t_new * IDX_HEADS, 1), lambda b, pt: (b, 0, 0)),
                  pl.BlockSpec((1, PAGE_SIZE, IDX_DIM), lambda b, pt: (b, 0, 0))]
                 + [page_spec(p) for p in range(n_pages)],
        out_specs=pl.BlockSpec((1, t_new, width), lambda b, pt: (b, 0, 0)),
    )
    return pl.pallas_call(
        functools.partial(_sample_scores_kernel, n_pages, t_new),
        grid_spec=grid_spec,
        out_shape=jax.ShapeDtypeStruct((nb, t_new, width), F32),
        compiler_params=pltpu.CompilerParams(dimension_semantics=("arbitrary",),
                                             vmem_limit_bytes=VMEM_LIMIT),
        name="sample_scores",
    )(page_table, qi_rows, w_col, ki_new_pad, *([cache_kidx_t] * n_pages))


def _sample_select_kernel(topk, sc_ref, sel_ref):
    def count_ge(cand):
        return jnp.sum((sc_ref[...] >= cand).astype(F32), axis=1, keepdims=True)

    thr = _kth_largest(count_ge, topk, (sc_ref.shape[0], 1))
    sc = sc_ref[...]
    sel_ref[...] = jnp.where((sc >= thr) & (sc > NEG_INF), 1.0, 0.0).astype(F32)


def _sample_select(scores2d, *, topk):
    return pl.pallas_call(
        functools.partial(_sample_select_kernel, topk),
        out_shape=jax.ShapeDtypeStruct(scores2d.shape, F32),
        compiler_params=pltpu.CompilerParams(vmem_limit_bytes=VMEM_LIMIT),
        name="sample_select",
    )(scores2d)


def _expand_matrix():
    e = np.zeros((2 * PAGE_SIZE, PAGE_SIZE * N_HEADS), np.float32)
    c = np.arange(PAGE_SIZE * N_HEADS)
    e[c // N_HEADS, c] = 1.0
    e[PAGE_SIZE + c % N_HEADS, c] = 1.0
    return jnp.asarray(e, BF16)


def _sample_attn_kernel(n_steps, g_pages, t_new, pt_ref, q_ref, e_ref, *refs):
    k_refs, v_refs = refs[:g_pages], refs[g_pages:2 * g_pages]
    sel_ref, kn_ref, vn_ref, seln_ref, a_ref, m_s, l_s, acc_s = refs[2 * g_pages:]
    step = pl.program_id(1)
    rows = t_new * N_HEADS
    cols = PAGE_SIZE * N_HEADS

    @pl.when(step == 0)
    def _():
        m_s[...] = jnp.full(m_s.shape, M_INIT, F32)
        l_s[...] = jnp.zeros(l_s.shape, F32)
        acc_s[...] = jnp.zeros(acc_s.shape, F32)

    q = q_ref[0]
    head_row = lax.broadcasted_iota(I32, (rows, LANES), 0) % N_HEADS
    head_onehot = (lax.broadcasted_iota(I32, (rows, LANES), 1) == head_row).astype(BF16)

    def attend(ks, vs, sels):
        n = len(ks)
        marks = []
        for sel in sels:
            per_row = jnp.concatenate(
                [jnp.broadcast_to(sel[t:t + 1, :], (N_HEADS, PAGE_SIZE)) for t in range(t_new)], axis=0)
            marks.append(jnp.concatenate([per_row.astype(BF16), head_onehot], axis=1))
        valid = _dot(jnp.concatenate(marks, axis=0), e_ref[...]) > 1.5
        s = jnp.concatenate(
            [jnp.where(valid[g * rows:(g + 1) * rows], _dot_nt(q, ks[g]), NEG_INF) for g in range(n)],
            axis=1)
        m_old = m_s[...]
        m_new = jnp.maximum(m_old, jnp.max(s, axis=1, keepdims=True))
        alpha = jnp.exp(m_old - m_new)
        pr = jnp.exp(s - m_new)
        l_s[...] = alpha * l_s[...] + jnp.sum(pr, axis=1, keepdims=True)
        pv = _dot(pr[:, :cols].astype(BF16), vs[0])
        for g in range(1, n):
            pv = pv + _dot(pr[:, g * cols:(g + 1) * cols].astype(BF16), vs[g])
        acc_s[...] = alpha * acc_s[...] + pv
        m_s[...] = m_new

    attend([r[0].astype(BF16) for r in k_refs], [r[0].astype(BF16) for r in v_refs],
           [sel_ref[0, :, g * PAGE_SIZE:(g + 1) * PAGE_SIZE] for g in range(g_pages)])

    @pl.when(step == n_steps - 1)
    def _():
        attend([kn_ref[0]], [vn_ref[0]], [seln_ref[0]])
        a_ref[0] = acc_s[...] / l_s[...]


def _sample_attention(page_table, q_rows, sel_pad, k_new_pad, v_new_pad, cache_k2, cache_v2, *, t_new):
    nb, n_pages = page_table.shape
    g_pages = PAGES_PER_STEP
    n_steps = n_pages // g_pages
    rows = t_new * N_HEADS
    cols = PAGE_SIZE * N_HEADS
    page = lambda g: pl.BlockSpec((1, cols, HEAD_DIM), lambda b, s, pt, g=g: (pt[b, s * g_pages + g], 0, 0))
    per_b = lambda r, n: pl.BlockSpec((1, r, n), lambda b, s, pt: (b, 0, 0))
    grid_spec = pltpu.PrefetchScalarGridSpec(
        num_scalar_prefetch=1,
        grid=(nb, n_steps),
        in_specs=[per_b(rows, HEAD_DIM),
                  pl.BlockSpec((2 * PAGE_SIZE, cols), lambda b, s, pt: (0, 0))]
                 + [page(g) for g in range(g_pages)] + [page(g) for g in range(g_pages)]
                 + [pl.BlockSpec((1, SUBLANES, g_pages * PAGE_SIZE), lambda b, s, pt: (b, 0, s)),
                    per_b(cols, HEAD_DIM), per_b(cols, HEAD_DIM),
                    pl.BlockSpec((1, SUBLANES, PAGE_SIZE), lambda b, s, pt: (b, 0, n_pages))],
        out_specs=per_b(rows, HEAD_DIM),
        scratch_shapes=[pltpu.VMEM((rows, 1), F32), pltpu.VMEM((rows, 1), F32),
                        pltpu.VMEM((rows, HEAD_DIM), F32)],
    )
    return pl.pallas_call(
        functools.partial(_sample_attn_kernel, n_steps, g_pages, t_new),
        grid_spec=grid_spec,
        out_shape=jax.ShapeDtypeStruct((nb, rows, HEAD_DIM), F32),
        compiler_params=pltpu.CompilerParams(dimension_semantics=("arbitrary", "arbitrary"),
                                             vmem_limit_bytes=VMEM_LIMIT),
        name="sample_attn",
    )(page_table, q_rows, _expand_matrix(), *([cache_k2] * g_pages), *([cache_v2] * g_pages),
      sel_pad, k_new_pad, v_new_pad, sel_pad)


def _sample_combine_kernel(t_new, a_ref, ga_ref, gb_ref, ext_ref, x_ref,
                           wp_ref, ps_ref, wo_ref, g_ref, b_ref, y_ref):
    p_t = []
    for t in range(t_new):
        parts = []
        for g, w in enumerate(POOL_WINDOWS):
            cols = slice(g * POOL_GROUP_DIM, (g + 1) * POOL_GROUP_DIM)
            cur = POOL_STATE + t
            win = ext_ref[cur, :, cols]
            for j in range(1, w):
                win = win + ext_ref[cur - j, :, cols]
            pooled = win / float(w) - ext_ref[cur, :, cols]
            parts.append(_dot(pooled.astype(BF16), wp_ref[g]))
        p_t.append(jnp.concatenate(parts, axis=-1) * ps_ref[...])
    y_ref[...] = _gate_out_norm(a_ref[...], ga_ref[...], jnp.concatenate(p_t, axis=0), gb_ref[...],
                                x_ref[...], wo_ref, g_ref, b_ref)


def _sample_combine(a, ga, gb, ext_tm, x2d, w_pool_b, pool_scale, w_out_b, ln_g, ln_b, *, t_new):
    return pl.pallas_call(
        functools.partial(_sample_combine_kernel, t_new),
        out_shape=jax.ShapeDtypeStruct(x2d.shape, F32),
        compiler_params=pltpu.CompilerParams(vmem_limit_bytes=VMEM_LIMIT),
        name="sample_combine",
    )(a, ga, gb, ext_tm, x2d, w_pool_b, pool_scale, w_out_b, ln_g, ln_b)


def kernel(x_prompt, x_sample, cache_k, cache_v, cache_kidx, state_pool, page_table,
           w_in, w_pool, pool_scale, w_out, ln_g, ln_b):
    assert w_in.shape[0] == DEPTH == 1
    batch, seq, _ = x_prompt.shape
    nb, t_new, _ = x_sample.shape
    n_pool, n_pages = cache_k.shape[1], page_table.shape[1]
    past = n_pages * PAGE_SIZE

    w_t = jnp.swapaxes(w_in[0], 0, 1).astype(BF16)
    w_pool_b = w_pool[0].astype(BF16)
    w_out_b = w_out[0].astype(BF16)
    ps, g, b = pool_scale[0][None, :], ln_g[0][None, :], ln_b[0][None, :]

    xp2d = x_prompt.reshape(batch * seq, D_MODEL)
    (q, kf, kb, vf, vb, ga, qi_hm, _, kib, kit, wit, u, gb) = _project(
        xp2d, w_t, jnp.arange(seq), tm=256, rows_per_seq=seq, head_major=True)
    a = _prompt_attention(q, qi_hm, wit, kib, kb, vb, batch=batch, seq=seq)
    y_prompt = _combine(a, ga, gb, u, xp2d, w_pool_b, ps, w_out_b, g, b, seq=seq, tm=256)

    rows_s = nb * t_new
    xs2d = x_sample.reshape(rows_s, D_MODEL)
    (qs, kfs, kbs, vfs, vbs, gas, qis, kifs, kibs, _, wits, us, gbs) = _project(
        xs2d, w_t, jnp.tile(past + jnp.arange(t_new), nb), tm=rows_s, rows_per_seq=rows_s,
        head_major=False)

    per_seq = lambda z, r, n: z.reshape(nb, r, n)
    pad_rows = lambda z, n: jnp.pad(z, ((0, 0), (0, n - z.shape[1]), (0, 0)))
    scores = _sample_scores(
        page_table,
        per_seq(qis, t_new * IDX_HEADS, IDX_DIM),
        per_seq(wits.T, t_new * IDX_HEADS, 1),
        pad_rows(per_seq(kibs, t_new, IDX_DIM), PAGE_SIZE),
        jnp.swapaxes(cache_kidx[0], 1, 2),
        t_new=t_new)
    width = scores.shape[-1]
    sel = _sample_select(scores.reshape(rows_s, width), topk=min(TOPK_MAX, (past + t_new) // 4))
    head_rows = t_new * N_HEADS
    a_s = _sample_attention(
        page_table,
        per_seq(qs, head_rows, HEAD_DIM),
        pad_rows(per_seq(sel, t_new, width), SUBLANES),
        pad_rows(per_seq(kbs, head_rows, HEAD_DIM), PAGE_SIZE * N_HEADS),
        pad_rows(per_seq(vbs, head_rows, HEAD_DIM), PAGE_SIZE * N_HEADS),
        cache_k.reshape(n_pool, PAGE_SIZE * N_HEADS, HEAD_DIM),
        cache_v.reshape(n_pool, PAGE_SIZE * N_HEADS, HEAD_DIM),
        t_new=t_new).reshape(rows_s, ATTN_WIDTH)

    u_ext = jnp.concatenate([state_pool[0], us.reshape(nb, t_new, POOL_WIDTH)], axis=1)
    time_major = lambda z: z.reshape(nb, t_new, z.shape[-1]).transpose(1, 0, 2).reshape(rows_s, z.shape[-1])
    y_tm = _sample_combine(time_major(a_s), time_major(gas), time_major(gbs), u_ext.transpose(1, 0, 2),
                           time_major(xs2d), w_pool_b, ps, w_out_b, g, b, t_new=t_new)
    y_sample = y_tm.reshape(t_new, nb, D_MODEL).transpose(1, 0, 2)

    hd = (N_HEADS, HEAD_DIM)
    return (y_prompt.reshape(batch, seq, D_MODEL),
            y_sample,
            kf.reshape(1, batch, seq, *hd), vf.reshape(1, batch, seq, *hd),
            jnp.swapaxes(kit, 1, 2)[None],
            u.reshape(batch, seq, POOL_WIDTH)[None, :, -POOL_STATE:],
            kfs.reshape(1, nb, t_new, *hd), vfs.reshape(1, nb, t_new, *hd),
            kifs.reshape(1, nb, t_new, IDX_DIM),
            u_ext[None, :, -POOL_STATE:])
```

```python
import functools

import numpy as np
import jax
import jax.numpy as jnp
from jax import lax
from jax.experimental import pallas as pl
from jax.experimental.pallas import tpu as pltpu

F32 = jnp.float32
BF16 = jnp.bfloat16
I32 = jnp.int32

D_MODEL = 2048
PAGE_SIZE = 128
N_HEADS = 8
HEAD_DIM = 128
ATTN_WIDTH = N_HEADS * HEAD_DIM
ROT_DIM = HEAD_DIM // 4
ROPE_THETA = 500000.0
IDX_HEADS = 16
IDX_DIM = 64
IDX_ROT_DIM = IDX_DIM // 4
TOPK_MAX = 256
POOL_WINDOWS = (2, 4, 8, 16)
N_POOL_GROUPS = len(POOL_WINDOWS)
POOL_WIDTH = D_MODEL - ATTN_WIDTH
POOL_GROUP_DIM = POOL_WIDTH // N_POOL_GROUPS
POOL_STATE = max(POOL_WINDOWS) - 1
IN_SPLITS = (ATTN_WIDTH, ATTN_WIDTH, ATTN_WIDTH, ATTN_WIDTH,
             IDX_HEADS * IDX_DIM, IDX_DIM, IDX_HEADS, POOL_WIDTH, POOL_WIDTH)
IN_COLS = sum(IN_SPLITS)
DEPTH = 1
DEEPNORM_ALPHA = (2 * DEPTH) ** 0.25
LN_EPS = 1e-5
ATTN_SCALE = HEAD_DIM ** -0.5
IDX_SCALE = (IDX_DIM ** -0.5) * (IDX_HEADS ** -0.5)

LANES = 128
SUBLANES = 8
VMEM_LIMIT = 56 * 1024 * 1024

(_OFF_Q, _OFF_K, _OFF_V, _OFF_GA, _OFF_QI, _OFF_KI, _OFF_WI, _OFF_U, _OFF_GB) = (
    [0] + np.cumsum(IN_SPLITS)[:-1].tolist())

NEG_INF = float("-inf")
M_INIT = -1e30
INT_MIN = -2 ** 31
KEY_OF_NEG_INF = -2139095041

KEY_CHUNK = 256
Q_TILE = 128
PAGES_PER_STEP = 8


def _dot(a, b):
    return jnp.dot(a, b, preferred_element_type=F32)


def _dot_nt(a, b):
    return lax.dot_general(a, b, (((1,), (1,)), ((), ())), preferred_element_type=F32)


def _sigmoid(x):
    return 1.0 / (1.0 + jnp.exp(-x))


def _rope_tables_t(pos, rot):
    half = rot // 2
    freqs = ROPE_THETA ** (-jnp.arange(half, dtype=F32) * (2.0 / rot))
    ang = pos.astype(F32)[:, None] * freqs[None, :]
    return jnp.stack([jnp.cos(ang).T, jnp.sin(ang).T]).astype(F32)


def _rope_tables(pos, rot, group):
    half = rot // 2
    freqs = ROPE_THETA ** (-jnp.arange(half, dtype=F32) * (2.0 / rot))
    ang = pos.astype(F32)[:, None] * freqs[None, :]
    cos, sin = jnp.cos(ang), jnp.sin(ang)
    lane = np.arange(LANES) % group
    sel = lane % half
    cos_l, sin_l = cos[:, sel], sin[:, sel]
    first = jnp.asarray(lane < half)[None, :]
    second = jnp.asarray((lane >= half) & (lane < rot))[None, :]
    c = jnp.where(first | second, cos_l, 1.0)
    s1 = jnp.where(first, -sin_l, 0.0)
    s2 = jnp.where(second, sin_l, 0.0)
    return jnp.stack([c, s1, s2]).astype(F32)


def _rope(x, tab_ref, half):
    fwd = pltpu.roll(x, LANES - half, 1)
    bwd = pltpu.roll(x, half, 1)
    return x * tab_ref[0] + fwd * tab_ref[1] + bwd * tab_ref[2]


def _proj_kernel(head_major, x_ref, wt_ref, th_ref, ti_ref, tit_ref,
                 q_ref, kf_ref, kb_ref, vf_ref, vb_ref, ga_ref, qi_ref, kif_ref, kib_ref, kit_ref,
                 wit_ref, u_ref, gb_ref):
    xb = x_ref[...].astype(BF16)

    def seg(off, n):
        return _dot_nt(xb, wt_ref[off:off + n, :])

    def seg_t(off, n):
        return _dot_nt(wt_ref[off:off + n, :], xb)

    for h in range(N_HEADS):
        sl = slice(h * HEAD_DIM, (h + 1) * HEAD_DIM)
        qh = _rope(seg(_OFF_Q + h * HEAD_DIM, HEAD_DIM), th_ref, ROT_DIM // 2)
        q_ref[:, sl] = (qh * ATTN_SCALE).astype(BF16)
        kh = _rope(seg(_OFF_K + h * HEAD_DIM, HEAD_DIM), th_ref, ROT_DIM // 2)
        kf_ref[:, sl] = kh
        kb_ref[:, sl] = kh.astype(BF16)

    v = seg(_OFF_V, ATTN_WIDTH)
    vf_ref[...] = v
    if head_major:
        vb_ref[0] = v.T.astype(BF16)
    else:
        vb_ref[...] = v.astype(BF16)
    ga_ref[...] = seg(_OFF_GA, ATTN_WIDTH)
    u_ref[...] = seg(_OFF_U, POOL_WIDTH)
    gb_ref[...] = seg(_OFF_GB, POOL_WIDTH)

    for j in range(IDX_HEADS // 2):
        r = _rope(seg(_OFF_QI + j * LANES, LANES), ti_ref, IDX_ROT_DIM // 2).astype(BF16)
        if head_major:
            qi_ref[2 * j] = r[:, :IDX_DIM]
            qi_ref[2 * j + 1] = r[:, IDX_DIM:]
        else:
            qi_ref[:, j * LANES:(j + 1) * LANES] = r

    half = IDX_ROT_DIM // 2
    kt = seg_t(_OFF_KI, IDX_DIM)
    x1, x2 = kt[:half], kt[half:2 * half]
    cos_t, sin_t = tit_ref[0], tit_ref[1]
    kit_ref[0] = jnp.concatenate([x1 * cos_t - x2 * sin_t, x1 * sin_t + x2 * cos_t, kt[2 * half:]], axis=0)
    kn = jnp.concatenate([seg(_OFF_KI, IDX_DIM), jnp.zeros((xb.shape[0], LANES - IDX_DIM), F32)], axis=1)
    ki = _rope(kn, ti_ref, half)[:, :IDX_DIM]
    kif_ref[...] = ki
    kib_ref[...] = ki.astype(BF16)

    wit_ref[...] = seg_t(_OFF_WI, IDX_HEADS) * IDX_SCALE


def _project(x2d, w_t, pos, *, tm, rows_per_seq, head_major):
    rows = x2d.shape[0]
    nb = rows // tm
    pos_blocks = pos.shape[0] // tm
    seq_blocks = rows_per_seq // tm
    tab_h = _rope_tables(pos, ROT_DIM, HEAD_DIM)
    tab_i = _rope_tables(pos, IDX_ROT_DIM, IDX_DIM)
    tab_it = _rope_tables_t(pos, IDX_ROT_DIM)

    row_blk = lambda n: pl.BlockSpec((tm, n), lambda i: (i, 0))
    tab_blk = pl.BlockSpec((3, tm, LANES), lambda i: (0, i % pos_blocks, 0))
    tabt_blk = pl.BlockSpec((2, IDX_ROT_DIM // 2, tm), lambda i: (0, 0, i % pos_blocks))
    wide = lambda dt: jax.ShapeDtypeStruct((rows, ATTN_WIDTH), dt)
    if head_major:
        qi_shape = jax.ShapeDtypeStruct((IDX_HEADS, rows, IDX_DIM), BF16)
        qi_blk = pl.BlockSpec((IDX_HEADS, tm, IDX_DIM), lambda i: (0, i, 0))
        vb_shape = jax.ShapeDtypeStruct((nb, ATTN_WIDTH, tm), BF16)
        vb_blk = pl.BlockSpec((1, ATTN_WIDTH, tm), lambda i: (i, 0, 0))
    else:
        qi_shape = jax.ShapeDtypeStruct((rows, IDX_HEADS * IDX_DIM), BF16)
        qi_blk = row_blk(IDX_HEADS * IDX_DIM)
        vb_shape, vb_blk = wide(BF16), row_blk(ATTN_WIDTH)

    out_shape = (wide(BF16), wide(F32), wide(BF16), wide(F32), vb_shape, wide(F32),
                 qi_shape,
                 jax.ShapeDtypeStruct((rows, IDX_DIM), F32), jax.ShapeDtypeStruct((rows, IDX_DIM), BF16),
                 jax.ShapeDtypeStruct((rows // rows_per_seq, IDX_DIM, rows_per_seq), F32),
                 jax.ShapeDtypeStruct((IDX_HEADS, rows), F32), wide(F32), wide(F32))
    out_specs = (row_blk(ATTN_WIDTH),) * 4 + (vb_blk, row_blk(ATTN_WIDTH)) + (
        qi_blk, row_blk(IDX_DIM), row_blk(IDX_DIM),
        pl.BlockSpec((1, IDX_DIM, tm), lambda i: (i // seq_blocks, 0, i % seq_blocks)),
        pl.BlockSpec((IDX_HEADS, tm), lambda i: (0, i)),
        row_blk(POOL_WIDTH), row_blk(POOL_WIDTH))
    return pl.pallas_call(
        functools.partial(_proj_kernel, head_major),
        grid=(nb,),
        in_specs=[row_blk(D_MODEL),
                  pl.BlockSpec((IN_COLS, D_MODEL), lambda i: (0, 0), pipeline_mode=pl.Buffered(1)),
                  tab_blk, tab_blk, tabt_blk],
        out_specs=out_specs,
        out_shape=out_shape,
        compiler_params=pltpu.CompilerParams(dimension_semantics=("arbitrary",),
                                             vmem_limit_bytes=VMEM_LIMIT),
        name="proj_hm" if head_major else "proj_rm",
    )(x2d, w_t, tab_h, tab_i, tab_it)


def _key_to_float(key):
    return pltpu.bitcast(key ^ ((key >> 31) & 0x7FFFFFFF), F32)


def _kth_largest(count_ge, topk, shape):
    def bit_body(it, prefix):
        trial = prefix | jnp.left_shift(jnp.int32(1), 31 - it)
        cnt = count_ge(_key_to_float(trial ^ INT_MIN))
        return jnp.where(cnt >= topk, trial, prefix)

    prefix = lax.fori_loop(0, 32, bit_body, jnp.zeros(shape, I32))
    return _key_to_float(jnp.maximum(prefix ^ INT_MIN, KEY_OF_NEG_INF))


def _fold_rows(x, op):
    return op(x.reshape(x.shape[0] // SUBLANES, SUBLANES, x.shape[1]), axis=0)


def _prompt_attn_kernel(topk, q_ref, qi_ref, wit_ref, ki_ref, k_ref, vt_ref, a_ref,
                        sc_s, bias_s, acc_s):
    i = pl.program_id(1)
    n_chunks = (i * Q_TILE + Q_TILE + KEY_CHUNK - 1) // KEY_CHUNK
    n_pairs = (n_chunks + 1) // 2
    qpos = i * Q_TILE + lax.broadcasted_iota(I32, (1, Q_TILE), 1)
    row = lax.broadcasted_iota(I32, (KEY_CHUNK, 1), 0)

    def score_chunk(c, carry):
        kic = ki_ref[c]
        acc = jnp.zeros((KEY_CHUNK, Q_TILE), F32)
        for j in range(IDX_HEADS // 2):
            qpair = qi_ref[2 * j:2 * j + 2].reshape(2 * Q_TILE, IDX_DIM)
            st = _dot_nt(kic, qpair)
            acc = acc + jnp.maximum(st[:, :Q_TILE], 0.0) * wit_ref[2 * j:2 * j + 1, :]
            acc = acc + jnp.maximum(st[:, Q_TILE:], 0.0) * wit_ref[2 * j + 1:2 * j + 2, :]
        causal = (c * KEY_CHUNK + row) <= qpos
        sc_s[c] = jnp.where(causal, acc, NEG_INF)
        return carry

    lax.fori_loop(0, n_chunks, score_chunk, 0)

    @pl.when(n_chunks % 2 == 1)
    def _():
        sc_s[n_chunks] = jnp.full((KEY_CHUNK, Q_TILE), NEG_INF, F32)

    def count_ge(cand):
        def body(c2, cnt):
            ge0 = (sc_s[2 * c2] >= cand).astype(F32)
            ge1 = (sc_s[2 * c2 + 1] >= cand).astype(F32)
            return cnt + _fold_rows(ge0, jnp.sum) + _fold_rows(ge1, jnp.sum)
        cnt = lax.fori_loop(0, n_pairs, body, jnp.zeros((SUBLANES, Q_TILE), F32))
        return jnp.sum(cnt, axis=0, keepdims=True)

    thr = _kth_largest(count_ge, topk, (1, Q_TILE))

    def bias_chunk(c, carry):
        sel = (sc_s[c] >= thr) & ((c * KEY_CHUNK + row) <= qpos)
        bias_s[c] = jnp.where(sel, 0.0, NEG_INF).astype(F32)
        return carry

    lax.fori_loop(0, n_chunks, bias_chunk, 0)

    def scores_t(c, h):
        sl = slice(h * HEAD_DIM, (h + 1) * HEAD_DIM)
        return _dot_nt(k_ref[c, :, sl], q_ref[:, sl]) + bias_s[c]

    def max_chunk(c, ms):
        return tuple(jnp.maximum(ms[h], _fold_rows(scores_t(c, h), jnp.max)) for h in range(N_HEADS))

    ms = lax.fori_loop(0, n_chunks, max_chunk,
                       tuple(jnp.full((SUBLANES, Q_TILE), M_INIT, F32) for _ in range(N_HEADS)))
    m = [jnp.max(x, axis=0, keepdims=True) for x in ms]

    acc_s[...] = jnp.zeros(acc_s.shape, F32)

    def value_chunk(c, ls):
        new_ls = []
        for h in range(N_HEADS):
            p = jnp.exp(scores_t(c, h) - m[h])
            new_ls.append(ls[h] + _fold_rows(p, jnp.sum))
            acc_s[h] += _dot(vt_ref[c, h * HEAD_DIM:(h + 1) * HEAD_DIM, :], p.astype(BF16))
        return tuple(new_ls)

    ls = lax.fori_loop(0, n_chunks, value_chunk,
                       tuple(jnp.zeros((SUBLANES, Q_TILE), F32) for _ in range(N_HEADS)))

    for h in range(N_HEADS):
        out_t = acc_s[h] / jnp.sum(ls[h], axis=0, keepdims=True)
        a_ref[:, h * HEAD_DIM:(h + 1) * HEAD_DIM] = out_t.T


def _prompt_attention(q, qi_hm, wit, ki_b, k_b, vt3, *, batch, seq):
    nq = seq // Q_TILE
    nch = seq // KEY_CHUNK
    topk = min(TOPK_MAX, seq // 4)
    ki3 = ki_b.reshape(batch * nch, KEY_CHUNK, IDX_DIM)
    k3 = k_b.reshape(batch * nch, KEY_CHUNK, ATTN_WIDTH)
    resident = lambda r, n: pl.BlockSpec((nch, r, n), lambda b, i: (b, 0, 0),
                                         pipeline_mode=pl.Buffered(1))
    return pl.pallas_call(
        functools.partial(_prompt_attn_kernel, topk),
        grid=(batch, nq),
        in_specs=[pl.BlockSpec((Q_TILE, ATTN_WIDTH), lambda b, i: (b * nq + i, 0)),
                  pl.BlockSpec((IDX_HEADS, Q_TILE, IDX_DIM), lambda b, i: (0, b * nq + i, 0)),
                  pl.BlockSpec((IDX_HEADS, Q_TILE), lambda b, i: (0, b * nq + i)),
                  resident(KEY_CHUNK, IDX_DIM), resident(KEY_CHUNK, ATTN_WIDTH),
                  resident(ATTN_WIDTH, KEY_CHUNK)],
        out_specs=pl.BlockSpec((Q_TILE, ATTN_WIDTH), lambda b, i: (b * nq + i, 0)),
        out_shape=jax.ShapeDtypeStruct((batch * seq, ATTN_WIDTH), F32),
        scratch_shapes=[pltpu.VMEM((nch, KEY_CHUNK, Q_TILE), F32),
                        pltpu.VMEM((nch, KEY_CHUNK, Q_TILE), F32),
                        pltpu.VMEM((N_HEADS, HEAD_DIM, Q_TILE), F32)],
        compiler_params=pltpu.CompilerParams(dimension_semantics=("arbitrary", "arbitrary"),
                                             vmem_limit_bytes=VMEM_LIMIT),
        name="prompt_attn",
    )(q, qi_hm, wit, ki3, k3, vt3)


HALO = 16


def _gate_out_norm(a, ga, p, gb, x, wo_ref, g_ref, b_ref):
    mixed = jnp.concatenate([a * (ga * _sigmoid(ga)), p * (gb * _sigmoid(gb))], axis=-1)
    h = DEEPNORM_ALPHA * x + _dot(mixed.astype(BF16), wo_ref[...])
    mu = jnp.mean(h, axis=-1, keepdims=True)
    d = h - mu
    var = jnp.mean(d * d, axis=-1, keepdims=True)
    return d * lax.rsqrt(var + LN_EPS) * g_ref[...] + b_ref[...]


def _combine_kernel(blocks_per_seq, a_ref, ga_ref, gb_ref, u_ref, uh_ref, x_ref,
                    wp_ref, ps_ref, wo_ref, g_ref, b_ref, y_ref, ext_s):
    i = pl.program_id(0)
    tm = u_ref.shape[0]
    first = (i % blocks_per_seq) == 0
    ext_s[0:HALO, :] = jnp.where(first, 0.0, uh_ref[...])
    ext_s[HALO:HALO + tm, :] = u_ref[...]
    pos = (i % blocks_per_seq) * tm + lax.broadcasted_iota(I32, (tm, 1), 0)

    parts = []
    for g, w in enumerate(POOL_WINDOWS):
        cols = slice(g * POOL_GROUP_DIM, (g + 1) * POOL_GROUP_DIM)
        win = ext_s[HALO:HALO + tm, cols]
        for j in range(1, w):
            win = win + ext_s[HALO - j:HALO - j + tm, cols]
        count = jnp.minimum(pos + 1, w).astype(F32)
        pooled = win / count - u_ref[:, cols]
        parts.append(_dot(pooled.astype(BF16), wp_ref[g]))
    p = jnp.concatenate(parts, axis=-1) * ps_ref[...]
    y_ref[...] = _gate_out_norm(a_ref[...], ga_ref[...], p, gb_ref[...], x_ref[...],
                                wo_ref, g_ref, b_ref)


def _combine(a, ga, gb, u, x2d, w_pool_b, pool_scale, w_out_b, ln_g, ln_b, *, seq, tm):
    rows = a.shape[0]
    nb = rows // tm
    bps = seq // tm
    hpb = tm // HALO
    row_blk = lambda n: pl.BlockSpec((tm, n), lambda i: (i, 0))
    const = lambda shape: pl.BlockSpec(shape, lambda i: (0,) * len(shape))
    return pl.pallas_call(
        functools.partial(_combine_kernel, bps),
        grid=(nb,),
        in_specs=[row_blk(ATTN_WIDTH), row_blk(ATTN_WIDTH), row_blk(POOL_WIDTH), row_blk(POOL_WIDTH),
                  pl.BlockSpec((HALO, POOL_WIDTH), lambda i: (jnp.maximum(i * hpb - 1, 0), 0)),
                  row_blk(D_MODEL),
                  const((N_POOL_GROUPS, POOL_GROUP_DIM, POOL_GROUP_DIM)), const((1, POOL_WIDTH)),
                  const((D_MODEL, D_MODEL)), const((1, D_MODEL)), const((1, D_MODEL))],
        out_specs=row_blk(D_MODEL),
        out_shape=jax.ShapeDtypeStruct((rows, D_MODEL), F32),
        scratch_shapes=[pltpu.VMEM((HALO + tm, POOL_WIDTH), F32)],
        compiler_params=pltpu.CompilerParams(dimension_semantics=("arbitrary",),
                                             vmem_limit_bytes=VMEM_LIMIT),
        name="combine",
    )(a, ga, gb, u, u, x2d, w_pool_b, pool_scale, w_out_b, ln_g, ln_b)


def _sample_scores_kernel(n_pages, t_new, pt_ref, qi_ref, w_ref, kn_ref, *refs):
    page_refs, sc_ref = refs[:n_pages], refs[n_pages]
    qi = qi_ref[0]
    w = w_ref[0]

    def head_sum(s):
        r = jnp.maximum(s, 0.0) * w
        return jnp.sum(r.reshape(t_new, IDX_HEADS, s.shape[-1]), axis=1)

    for p in range(n_pages):
        kp_t = page_refs[p][0].astype(BF16)
        sc_ref[0, :, p * PAGE_SIZE:(p + 1) * PAGE_SIZE] = head_sum(_dot(qi, kp_t))
    s_new = head_sum(_dot_nt(qi, kn_ref[0]))
    t_idx = lax.broadcasted_iota(I32, s_new.shape, 0)
    j_idx = lax.broadcasted_iota(I32, s_new.shape, 1)
    sc_ref[0, :, n_pages * PAGE_SIZE:] = jnp.where(j_idx <= t_idx, s_new, NEG_INF)


def _sample_scores(page_table, qi_rows, w_col, ki_new_pad, cache_kidx_t, *, t_new):
    nb, n_pages = page_table.shape
    width = (n_pages + 1) * PAGE_SIZE
    page_spec = lambda p: pl.BlockSpec((1, IDX_DIM, PAGE_SIZE), lambda b, pt, p=p: (pt[b, p], 0, 0))
    grid_spec = pltpu.PrefetchScalarGridSpec(
        num_scalar_prefetch=1,
        grid=(nb,),
        in_specs=[pl.BlockSpec((1, t_new * IDX_HEADS, IDX_DIM), lambda b, pt: (b, 0, 0)),
                  pl.BlockSpec((1, t_new * IDX_HEADS, 1), lambda b, pt: (b, 0, 0)),
                  pl.BlockSpec((1, PAGE_SIZE, IDX_DIM), lambda b, pt: (b, 0, 0))]
                 + [page_spec(p) for p in range(n_pages)],
        out_specs=pl.BlockSpec((1, t_new, width), lambda b, pt: (b, 0, 0)),
    )
    return pl.pallas_call(
        functools.partial(_sample_scores_kernel, n_pages, t_new),
        grid_spec=grid_spec,
        out_shape=jax.ShapeDtypeStruct((nb, t_new, width), F32),
        compiler_params=pltpu.CompilerParams(dimension_semantics=("arbitrary",),
                                             vmem_limit_bytes=VMEM_LIMIT),
        name="sample_scores",
    )(page_table, qi_rows, w_col, ki_new_pad, *([cache_kidx_t] * n_pages))


def _sample_select_kernel(topk, sc_ref, sel_ref):
    def count_ge(cand):
        return jnp.sum((sc_ref[...] >= cand).astype(F32), axis=1, keepdims=True)

    thr = _kth_largest(count_ge, topk, (sc_ref.shape[0], 1))
    sc = sc_ref[...]
    sel_ref[...] = jnp.where((sc >= thr) & (sc > NEG_INF), 1.0, 0.0).astype(F32)


def _sample_select(scores2d, *, topk):
    return pl.pallas_call(
        functools.partial(_sample_select_kernel, topk),
        out_shape=jax.ShapeDtypeStruct(scores2d.shape, F32),
        compiler_params=pltpu.CompilerParams(vmem_limit_bytes=VMEM_LIMIT),
        name="sample_select",
    )(scores2d)


def _expand_matrix():
    e = np.zeros((2 * PAGE_SIZE, PAGE_SIZE * N_HEADS), np.float32)
    c = np.arange(PAGE_SIZE * N_HEADS)
    e[c // N_HEADS, c] = 1.0
    e[PAGE_SIZE + c % N_HEADS, c] = 1.0
    return jnp.asarray(e, BF16)


def _sample_attn_kernel(n_steps, g_pages, t_new, pt_ref, q_ref, e_ref, *refs):
    k_refs, v_refs = refs[:g_pages], refs[g_pages:2 * g_pages]
    sel_ref, kn_ref, vn_ref, seln_ref, a_ref, m_s, l_s, acc_s = refs[2 * g_pages:]
    step = pl.program_id(1)
    rows = t_new * N_HEADS
    cols = PAGE_SIZE * N_HEADS

    @pl.when(step == 0)
    def _():
        m_s[...] = jnp.full(m_s.shape, M_INIT, F32)
        l_s[...] = jnp.zeros(l_s.shape, F32)
        acc_s[...] = jnp.zeros(acc_s.shape, F32)

    q = q_ref[0]
    head_row = lax.broadcasted_iota(I32, (rows, LANES), 0) % N_HEADS
    head_onehot = (lax.broadcasted_iota(I32, (rows, LANES), 1) == head_row).astype(BF16)

    def attend(ks, vs, sels):
        n = len(ks)
        marks = []
        for sel in sels:
            per_row = jnp.concatenate(
                [jnp.broadcast_to(sel[t:t + 1, :], (N_HEADS, PAGE_SIZE)) for t in range(t_new)], axis=0)
            marks.append(jnp.concatenate([per_row.astype(BF16), head_onehot], axis=1))
        valid = _dot(jnp.concatenate(marks, axis=0), e_ref[...]) > 1.5
        s = jnp.concatenate(
            [jnp.where(valid[g * rows:(g + 1) * rows], _dot_nt(q, ks[g]), NEG_INF) for g in range(n)],
            axis=1)
        m_old = m_s[...]
        m_new = jnp.maximum(m_old, jnp.max(s, axis=1, keepdims=True))
        alpha = jnp.exp(m_old - m_new)
        pr = jnp.exp(s - m_new)
        l_s[...] = alpha * l_s[...] + jnp.sum(pr, axis=1, keepdims=True)
        pv = _dot(pr[:, :cols].astype(BF16), vs[0])
        for g in range(1, n):
            pv = pv + _dot(pr[:, g * cols:(g + 1) * cols].astype(BF16), vs[g])
        acc_s[...] = alpha * acc_s[...] + pv
        m_s[...] = m_new

    attend([r[0].astype(BF16) for r in k_refs], [r[0].astype(BF16) for r in v_refs],
           [sel_ref[0, :, g * PAGE_SIZE:(g + 1) * PAGE_SIZE] for g in range(g_pages)])

    @pl.when(step == n_steps - 1)
    def _():
        attend([kn_ref[0]], [vn_ref[0]], [seln_ref[0]])
        a_ref[0] = acc_s[...] / l_s[...]


def _sample_attention(page_table, q_rows, sel_pad, k_new_pad, v_new_pad, cache_k2, cache_v2, *, t_new):
    nb, n_pages = page_table.shape
    g_pages = PAGES_PER_STEP
    n_steps = n_pages // g_pages
    rows = t_new * N_HEADS
    cols = PAGE_SIZE * N_HEADS
    page = lambda g: pl.BlockSpec((1, cols, HEAD_DIM), lambda b, s, pt, g=g: (pt[b, s * g_pages + g], 0, 0))
    per_b = lambda r, n: pl.BlockSpec((1, r, n), lambda b, s, pt: (b, 0, 0))
    grid_spec = pltpu.PrefetchScalarGridSpec(
        num_scalar_prefetch=1,
        grid=(nb, n_steps),
        in_specs=[per_b(rows, HEAD_DIM),
                  pl.BlockSpec((2 * PAGE_SIZE, cols), lambda b, s, pt: (0, 0))]
                 + [page(g) for g in range(g_pages)] + [page(g) for g in range(g_pages)]
                 + [pl.BlockSpec((1, SUBLANES, g_pages * PAGE_SIZE), lambda b, s, pt: (b, 0, s)),
                    per_b(cols, HEAD_DIM), per_b(cols, HEAD_DIM),
                    pl.BlockSpec((1, SUBLANES, PAGE_SIZE), lambda b, s, pt: (b, 0, n_pages))],
        out_specs=per_b(rows, HEAD_DIM),
        scratch_shapes=[pltpu.VMEM((rows, 1), F32), pltpu.VMEM((rows, 1), F32),
                        pltpu.VMEM((rows, HEAD_DIM), F32)],
    )
    return pl.pallas_call(
        functools.partial(_sample_attn_kernel, n_steps, g_pages, t_new),
        grid_spec=grid_spec,
        out_shape=jax.ShapeDtypeStruct((nb, rows, HEAD_DIM), F32),
        compiler_params=pltpu.CompilerParams(dimension_semantics=("arbitrary", "arbitrary"),
                                             vmem_limit_bytes=VMEM_LIMIT),
        name="sample_attn",
    )(page_table, q_rows, _expand_matrix(), *([cache_k2] * g_pages), *([cache_v2] * g_pages),
      sel_pad, k_new_pad, v_new_pad, sel_pad)


def _sample_combine_kernel(t_new, a_ref, ga_ref, gb_ref, ext_ref, x_ref,
                           wp_ref, ps_ref, wo_ref, g_ref, b_ref, y_ref):
    p_t = []
    for t in range(t_new):
        parts = []
        for g, w in enumerate(POOL_WINDOWS):
            cols = slice(g * POOL_GROUP_DIM, (g + 1) * POOL_GROUP_DIM)
            cur = POOL_STATE + t
            win = ext_ref[cur, :, cols]
            for j in range(1, w):
                win = win + ext_ref[cur - j, :, cols]
            pooled = win / float(w) - ext_ref[cur, :, cols]
            parts.append(_dot(pooled.astype(BF16), wp_ref[g]))
        p_t.append(jnp.concatenate(parts, axis=-1) * ps_ref[...])
    y_ref[...] = _gate_out_norm(a_ref[...], ga_ref[...], jnp.concatenate(p_t, axis=0), gb_ref[...],
                                x_ref[...], wo_ref, g_ref, b_ref)


def _sample_combine(a, ga, gb, ext_tm, x2d, w_pool_b, pool_scale, w_out_b, ln_g, ln_b, *, t_new):
    return pl.pallas_call(
        functools.partial(_sample_combine_kernel, t_new),
        out_shape=jax.ShapeDtypeStruct(x2d.shape, F32),
        compiler_params=pltpu.CompilerParams(vmem_limit_bytes=VMEM_LIMIT),
        name="sample_combine",
    )(a, ga, gb, ext_tm, x2d, w_pool_b, pool_scale, w_out_b, ln_g, ln_b)


def kernel(x_prompt, x_sample, cache_k, cache_v, cache_kidx, state_pool, page_table,
           w_in, w_pool, pool_scale, w_out, ln_g, ln_b):
    assert w_in.shape[0] == DEPTH == 1
    batch, seq, _ = x_prompt.shape
    nb, t_new, _ = x_sample.shape
    n_pool, n_pages = cache_k.shape[1], page_table.shape[1]
    past = n_pages * PAGE_SIZE

    w_t = jnp.swapaxes(w_in[0], 0, 1).astype(BF16)
    w_pool_b = w_pool[0].astype(BF16)
    w_out_b = w_out[0].astype(BF16)
    ps, g, b = pool_scale[0][None, :], ln_g[0][None, :], ln_b[0][None, :]

    xp2d = x_prompt.reshape(batch * seq, D_MODEL)
    (q, kf, kb, vf, vb, ga, qi_hm, _, kib, kit, wit, u, gb) = _project(
        xp2d, w_t, jnp.arange(seq), tm=256, rows_per_seq=seq, head_major=True)
    a = _prompt_attention(q, qi_hm, wit, kib, kb, vb, batch=batch, seq=seq)
    y_prompt = _combine(a, ga, gb, u, xp2d, w_pool_b, ps, w_out_b, g, b, seq=seq, tm=256)

    rows_s = nb * t_new
    xs2d = x_sample.reshape(rows_s, D_MODEL)
    (qs, kfs, kbs, vfs, vbs, gas, qis, kifs, kibs, _, wits, us, gbs) = _project(
        xs2d, w_t, jnp.tile(past + jnp.arange(t_new), nb), tm=rows_s, rows_per_seq=rows_s,
        head_major=False)

    per_seq = lambda z, r, n: z.reshape(nb, r, n)
    pad_rows = lambda z, n: jnp.pad(z, ((0, 0), (0, n - z.shape[1]), (0, 0)))
    scores = _sample_scores(
        page_table,
        per_seq(qis, t_new * IDX_HEADS, IDX_DIM),
        per_seq(wits.T, t_new * IDX_HEADS, 1),
        pad_rows(per_seq(kibs, t_new, IDX_DIM), PAGE_SIZE),
        jnp.swapaxes(cache_kidx[0], 1, 2),
        t_new=t_new)
    width = scores.shape[-1]
    sel = _sample_select(scores.reshape(rows_s, width), topk=min(TOPK_MAX, (past + t_new) // 4))
    head_rows = t_new * N_HEADS
    a_s = _sample_attention(
        page_table,
        per_seq(qs, head_rows, HEAD_DIM),
        pad_rows(per_seq(sel, t_new, width), SUBLANES),
        pad_rows(per_seq(kbs, head_rows, HEAD_DIM), PAGE_SIZE * N_HEADS),
        pad_rows(per_seq(vbs, head_rows, HEAD_DIM), PAGE_SIZE * N_HEADS),
        cache_k.reshape(n_pool, PAGE_SIZE * N_HEADS, HEAD_DIM),
        cache_v.reshape(n_pool, PAGE_SIZE * N_HEADS, HEAD_DIM),
        t_new=t_new).reshape(rows_s, ATTN_WIDTH)

    u_ext = jnp.concatenate([state_pool[0], us.reshape(nb, t_new, POOL_WIDTH)], axis=1)
    time_major = lambda z: z.reshape(nb, t_new, z.shape[-1]).transpose(1, 0, 2).reshape(rows_s, z.shape[-1])
    y_tm = _sample_combine(time_major(a_s), time_major(gas), time_major(gbs), u_ext.transpose(1, 0, 2),
                           time_major(xs2d), w_pool_b, ps, w_out_b, g, b, t_new=t_new)
    y_sample = y_tm.reshape(t_new, nb, D_MODEL).transpose(1, 0, 2)

    hd = (N_HEADS, HEAD_DIM)
    return (y_prompt.reshape(batch, seq, D_MODEL),
            y_sample,
            kf.reshape(1, batch, seq, *hd), vf.reshape(1, batch, seq, *hd),
            jnp.swapaxes(kit, 1, 2)[None],
            u.reshape(batch, seq, POOL_WIDTH)[None, :, -POOL_STATE:],
            kfs.reshape(1, nb, t_new, *hd), vfs.reshape(1, nb, t_new, *hd),
            kifs.reshape(1, nb, t_new, IDX_DIM),
            u_ext[None, :, -POOL_STATE:])
```

```python
import functools

import numpy as np
import jax
import jax.numpy as jnp
from jax import lax
from jax.experimental import pallas as pl
from jax.experimental.pallas import tpu as pltpu

F32 = jnp.float32
BF16 = jnp.bfloat16
I32 = jnp.int32

D_MODEL = 2048
PAGE_SIZE = 128
N_HEADS = 8
HEAD_DIM = 128
ATTN_WIDTH = N_HEADS * HEAD_DIM
ROT_DIM = HEAD_DIM // 4
ROPE_THETA = 500000.0
IDX_HEADS = 16
IDX_DIM = 64
IDX_ROT_DIM = IDX_DIM // 4
TOPK_MAX = 256
POOL_WINDOWS = (2, 4, 8, 16)
N_POOL_GROUPS = len(POOL_WINDOWS)
POOL_WIDTH = D_MODEL - ATTN_WIDTH
POOL_GROUP_DIM = POOL_WIDTH // N_POOL_GROUPS
POOL_STATE = max(POOL_WINDOWS) - 1
IN_SPLITS = (ATTN_WIDTH, ATTN_WIDTH, ATTN_WIDTH, ATTN_WIDTH,
             IDX_HEADS * IDX_DIM, IDX_DIM, IDX_HEADS, POOL_WIDTH, POOL_WIDTH)
IN_COLS = sum(IN_SPLITS)
DEPTH = 1
DEEPNORM_ALPHA = (2 * DEPTH) ** 0.25
LN_EPS = 1e-5
ATTN_SCALE = HEAD_DIM ** -0.5
IDX_SCALE = (IDX_DIM ** -0.5) * (IDX_HEADS ** -0.5)

LANES = 128
SUBLANES = 8
VMEM_LIMIT = 56 * 1024 * 1024

(_OFF_Q, _OFF_K, _OFF_V, _OFF_GA, _OFF_QI, _OFF_KI, _OFF_WI, _OFF_U, _OFF_GB) = (
    [0] + np.cumsum(IN_SPLITS)[:-1].tolist())

NEG_INF = float("-inf")
M_INIT = -1e30
INT_MIN = -2 ** 31
KEY_OF_NEG_INF = -2139095041

KEY_CHUNK = 256
HALF_CHUNK = 128
Q_TILE = 256
PAGES_PER_STEP = 16


def _dot(a, b):
    return jnp.dot(a, b, preferred_element_type=F32)


def _dot_nt(a, b):
    return lax.dot_general(a, b, (((1,), (1,)), ((), ())), preferred_element_type=F32)


def _sigmoid(x):
    return 1.0 / (1.0 + jnp.exp(-x))


def _rope_tables_t(pos, rot):
    half = rot // 2
    freqs = ROPE_THETA ** (-jnp.arange(half, dtype=F32) * (2.0 / rot))
    ang = pos.astype(F32)[:, None] * freqs[None, :]
    return jnp.stack([jnp.cos(ang).T, jnp.sin(ang).T]).astype(F32)


def _rope_tables(pos, rot, group):
    half = rot // 2
    freqs = ROPE_THETA ** (-jnp.arange(half, dtype=F32) * (2.0 / rot))
    ang = pos.astype(F32)[:, None] * freqs[None, :]
    cos, sin = jnp.cos(ang), jnp.sin(ang)
    lane = np.arange(LANES) % group
    sel = lane % half
    cos_l, sin_l = cos[:, sel], sin[:, sel]
    first = jnp.asarray(lane < half)[None, :]
    second = jnp.asarray((lane >= half) & (lane < rot))[None, :]
    c = jnp.where(first | second, cos_l, 1.0)
    s1 = jnp.where(first, -sin_l, 0.0)
    s2 = jnp.where(second, sin_l, 0.0)
    return jnp.stack([c, s1, s2]).astype(F32)


def _rope(x, tab_ref, half):
    fwd = pltpu.roll(x, LANES - half, 1)
    bwd = pltpu.roll(x, half, 1)
    return x * tab_ref[0] + fwd * tab_ref[1] + bwd * tab_ref[2]


def _proj_kernel(head_major, x_ref, wt_ref, th_ref, ti_ref, tit_ref,
                 q_ref, kf_ref, kb_ref, vf_ref, vb_ref, ga_ref, qi_ref, kif_ref, kib_ref, kit_ref,
                 wit_ref, u_ref, gb_ref):
    xb = x_ref[...].astype(BF16)

    def seg(off, n):
        return _dot_nt(xb, wt_ref[off:off + n, :])

    def seg_t(off, n):
        return _dot_nt(wt_ref[off:off + n, :], xb)

    qf, kf = seg(_OFF_Q, ATTN_WIDTH), seg(_OFF_K, ATTN_WIDTH)
    for h in range(N_HEADS):
        sl = slice(h * HEAD_DIM, (h + 1) * HEAD_DIM)
        qh = _rope(qf[:, sl], th_ref, ROT_DIM // 2)
        q_ref[:, sl] = (qh * ATTN_SCALE).astype(BF16)
        kh = _rope(kf[:, sl], th_ref, ROT_DIM // 2)
        kf_ref[:, sl] = kh
        kb_ref[:, sl] = kh.astype(BF16)

    v = seg(_OFF_V, ATTN_WIDTH)
    vf_ref[...] = v
    if head_major:
        vb_ref[0] = v.T.astype(BF16)
    else:
        vb_ref[...] = v.astype(BF16)
    ga_ref[...] = seg(_OFF_GA, ATTN_WIDTH)
    u_ref[...] = seg(_OFF_U, POOL_WIDTH)
    gb_ref[...] = seg(_OFF_GB, POOL_WIDTH)

    qif = seg(_OFF_QI, IDX_HEADS * IDX_DIM)
    for j in range(IDX_HEADS // 2):
        r = _rope(qif[:, j * LANES:(j + 1) * LANES], ti_ref, IDX_ROT_DIM // 2).astype(BF16)
        if head_major:
            qi_ref[2 * j] = r[:, :IDX_DIM]
            qi_ref[2 * j + 1] = r[:, IDX_DIM:]
        else:
            qi_ref[:, j * LANES:(j + 1) * LANES] = r

    half = IDX_ROT_DIM // 2
    kt = seg_t(_OFF_KI, IDX_DIM)
    x1, x2 = kt[:half], kt[half:2 * half]
    cos_t, sin_t = tit_ref[0], tit_ref[1]
    kit_ref[0] = jnp.concatenate([x1 * cos_t - x2 * sin_t, x1 * sin_t + x2 * cos_t, kt[2 * half:]], axis=0)
    kn = jnp.concatenate([seg(_OFF_KI, IDX_DIM), jnp.zeros((xb.shape[0], LANES - IDX_DIM), F32)], axis=1)
    ki = _rope(kn, ti_ref, half)[:, :IDX_DIM]
    kif_ref[...] = ki
    kib_ref[...] = ki.astype(BF16)

    wit_ref[...] = seg_t(_OFF_WI, IDX_HEADS) * IDX_SCALE


def _project(x2d, w_t, pos, *, tm, rows_per_seq, head_major):
    rows = x2d.shape[0]
    nb = rows // tm
    pos_blocks = pos.shape[0] // tm
    seq_blocks = rows_per_seq // tm
    tab_h = _rope_tables(pos, ROT_DIM, HEAD_DIM)
    tab_i = _rope_tables(pos, IDX_ROT_DIM, IDX_DIM)
    tab_it = _rope_tables_t(pos, IDX_ROT_DIM)

    row_blk = lambda n: pl.BlockSpec((tm, n), lambda i: (i, 0))
    tab_blk = pl.BlockSpec((3, tm, LANES), lambda i: (0, i % pos_blocks, 0))
    tabt_blk = pl.BlockSpec((2, IDX_ROT_DIM // 2, tm), lambda i: (0, 0, i % pos_blocks))
    wide = lambda dt: jax.ShapeDtypeStruct((rows, ATTN_WIDTH), dt)
    if head_major:
        qi_shape = jax.ShapeDtypeStruct((IDX_HEADS, rows, IDX_DIM), BF16)
        qi_blk = pl.BlockSpec((IDX_HEADS, tm, IDX_DIM), lambda i: (0, i, 0))
        vb_shape = jax.ShapeDtypeStruct((nb, ATTN_WIDTH, tm), BF16)
        vb_blk = pl.BlockSpec((1, ATTN_WIDTH, tm), lambda i: (i, 0, 0))
    else:
        qi_shape = jax.ShapeDtypeStruct((rows, IDX_HEADS * IDX_DIM), BF16)
        qi_blk = row_blk(IDX_HEADS * IDX_DIM)
        vb_shape, vb_blk = wide(BF16), row_blk(ATTN_WIDTH)

    out_shape = (wide(BF16), wide(F32), wide(BF16), wide(F32), vb_shape, wide(F32),
                 qi_shape,
                 jax.ShapeDtypeStruct((rows, IDX_DIM), F32), jax.ShapeDtypeStruct((rows, IDX_DIM), BF16),
                 jax.ShapeDtypeStruct((rows // rows_per_seq, IDX_DIM, rows_per_seq), F32),
                 jax.ShapeDtypeStruct((IDX_HEADS, rows), F32), wide(F32), wide(F32))
    out_specs = (row_blk(ATTN_WIDTH),) * 4 + (vb_blk, row_blk(ATTN_WIDTH)) + (
        qi_blk, row_blk(IDX_DIM), row_blk(IDX_DIM),
        pl.BlockSpec((1, IDX_DIM, tm), lambda i: (i // seq_blocks, 0, i % seq_blocks)),
        pl.BlockSpec((IDX_HEADS, tm), lambda i: (0, i)),
        row_blk(POOL_WIDTH), row_blk(POOL_WIDTH))
    return pl.pallas_call(
        functools.partial(_proj_kernel, head_major),
        grid=(nb,),
        in_specs=[row_blk(D_MODEL),
                  pl.BlockSpec((IN_COLS, D_MODEL), lambda i: (0, 0), pipeline_mode=pl.Buffered(1)),
                  tab_blk, tab_blk, tabt_blk],
        out_specs=out_specs,
        out_shape=out_shape,
        compiler_params=pltpu.CompilerParams(dimension_semantics=("arbitrary",),
                                             vmem_limit_bytes=VMEM_LIMIT),
        name="proj_hm" if head_major else "proj_rm",
    )(x2d, w_t, tab_h, tab_i, tab_it)


def _key_to_float(key):
    return pltpu.bitcast(key ^ ((key >> 31) & 0x7FFFFFFF), F32)


def _kth_largest(count_ge, topk, shape):
    def bit_body(it, prefix):
        trial = prefix | jnp.left_shift(jnp.int32(1), 31 - it)
        cnt = count_ge(_key_to_float(trial ^ INT_MIN))
        return jnp.where(cnt >= topk, trial, prefix)

    prefix = lax.fori_loop(0, 32, bit_body, jnp.zeros(shape, I32))
    return _key_to_float(jnp.maximum(prefix ^ INT_MIN, KEY_OF_NEG_INF))


def _fold_rows(x, op):
    return op(x.reshape(x.shape[0] // SUBLANES, SUBLANES, x.shape[1]), axis=0)


def _prompt_attn_kernel(topk, q_ref, qi_ref, wit_ref, ki_ref, k_ref, vt_ref, a_ref,
                        sc_s, bias_s, acc_s):
    i = pl.program_id(1)
    n_chunks = (i * Q_TILE + Q_TILE + KEY_CHUNK - 1) // KEY_CHUNK
    qpos = i * Q_TILE + lax.broadcasted_iota(I32, (1, Q_TILE), 1)
    row = lax.broadcasted_iota(I32, (HALF_CHUNK, 1), 0)
    halves = [slice(r * HALF_CHUNK, (r + 1) * HALF_CHUNK) for r in range(KEY_CHUNK // HALF_CHUNK)]

    def score_chunk(c, carry):
        for rows in halves:
            kic = ki_ref[c, rows, :]
            acc = jnp.zeros((HALF_CHUNK, Q_TILE), F32)
            for h in range(IDX_HEADS):
                st = _dot_nt(kic, qi_ref[h])
                acc = acc + jnp.maximum(st, 0.0) * wit_ref[h:h + 1, :]
            causal = (c * KEY_CHUNK + rows.start + row) <= qpos
            sc_s[c, rows, :] = jnp.where(causal, acc, NEG_INF)
        return carry

    lax.fori_loop(0, n_chunks, score_chunk, 0)

    def count_ge(cand):
        def body(c, cnt):
            for rows in halves:
                cnt = cnt + _fold_rows((sc_s[c, rows, :] >= cand).astype(F32), jnp.sum)
            return cnt
        cnt = lax.fori_loop(0, n_chunks, body, jnp.zeros((SUBLANES, Q_TILE), F32))
        return jnp.sum(cnt, axis=0, keepdims=True)

    thr = _kth_largest(count_ge, topk, (1, Q_TILE))

    def bias_chunk(c, carry):
        for rows in halves:
            sel = (sc_s[c, rows, :] >= thr) & ((c * KEY_CHUNK + rows.start + row) <= qpos)
            bias_s[c, rows, :] = jnp.where(sel, 0.0, NEG_INF).astype(F32)
        return carry

    lax.fori_loop(0, n_chunks, bias_chunk, 0)

    def scores_t(c, h, rows):
        sl = slice(h * HEAD_DIM, (h + 1) * HEAD_DIM)
        return _dot_nt(k_ref[c, rows, sl], q_ref[:, sl]) + bias_s[c, rows, :]

    def max_chunk(c, ms):
        new_ms = []
        for h in range(N_HEADS):
            mh = ms[h]
            for rows in halves:
                mh = jnp.maximum(mh, _fold_rows(scores_t(c, h, rows), jnp.max))
            new_ms.append(mh)
        return tuple(new_ms)

    ms = lax.fori_loop(0, n_chunks, max_chunk,
                       tuple(jnp.full((SUBLANES, Q_TILE), M_INIT, F32) for _ in range(N_HEADS)))
    m = [jnp.max(x, axis=0, keepdims=True) for x in ms]

    acc_s[...] = jnp.zeros(acc_s.shape, F32)

    lane_groups = [slice(g * LANES, (g + 1) * LANES) for g in range(Q_TILE // LANES)]

    def value_chunk(c, ls):
        new_ls = []
        for h in range(N_HEADS):
            sl = slice(h * HEAD_DIM, (h + 1) * HEAD_DIM)
            lh = []
            for qs in lane_groups:
                s = _dot_nt(k_ref[c, :, sl], q_ref[qs, sl]) + bias_s[c, :, qs]
                p = jnp.exp(s - m[h][:, qs])
                lh.append(_fold_rows(p, jnp.sum))
                acc_s[h, :, qs] += _dot(vt_ref[c, sl, :], p.astype(BF16))
            new_ls.append(ls[h] + jnp.concatenate(lh, axis=1))
        return tuple(new_ls)

    ls = lax.fori_loop(0, n_chunks, value_chunk,
                       tuple(jnp.zeros((SUBLANES, Q_TILE), F32) for _ in range(N_HEADS)))

    for h in range(N_HEADS):
        out_t = acc_s[h] / jnp.sum(ls[h], axis=0, keepdims=True)
        a_ref[:, h * HEAD_DIM:(h + 1) * HEAD_DIM] = out_t.T


def _prompt_attention(q, qi_hm, wit, ki_b, k_b, vt3, *, batch, seq):
    nq = seq // Q_TILE
    nch = seq // KEY_CHUNK
    topk = min(TOPK_MAX, seq // 4)
    ki3 = ki_b.reshape(batch * nch, KEY_CHUNK, IDX_DIM)
    k3 = k_b.reshape(batch * nch, KEY_CHUNK, ATTN_WIDTH)
    resident = lambda r, n: pl.BlockSpec((nch, r, n), lambda b, i: (b, 0, 0),
                                         pipeline_mode=pl.Buffered(1))
    return pl.pallas_call(
        functools.partial(_prompt_attn_kernel, topk),
        grid=(batch, nq),
        in_specs=[pl.BlockSpec((Q_TILE, ATTN_WIDTH), lambda b, i: (b * nq + i, 0)),
                  pl.BlockSpec((IDX_HEADS, Q_TILE, IDX_DIM), lambda b, i: (0, b * nq + i, 0)),
                  pl.BlockSpec((IDX_HEADS, Q_TILE), lambda b, i: (0, b * nq + i)),
                  resident(KEY_CHUNK, IDX_DIM), resident(KEY_CHUNK, ATTN_WIDTH),
                  resident(ATTN_WIDTH, KEY_CHUNK)],
        out_specs=pl.BlockSpec((Q_TILE, ATTN_WIDTH), lambda b, i: (b * nq + i, 0)),
        out_shape=jax.ShapeDtypeStruct((batch * seq, ATTN_WIDTH), F32),
        scratch_shapes=[pltpu.VMEM((nch, KEY_CHUNK, Q_TILE), F32),
                        pltpu.VMEM((nch, KEY_CHUNK, Q_TILE), F32),
                        pltpu.VMEM((N_HEADS, HEAD_DIM, Q_TILE), F32)],
        compiler_params=pltpu.CompilerParams(dimension_semantics=("arbitrary", "arbitrary"),
                                             vmem_limit_bytes=VMEM_LIMIT),
        name="prompt_attn",
    )(q, qi_hm, wit, ki3, k3, vt3)


HALO = 16


def _gate_out_norm(a, ga, p, gb, x, wo_ref, g_ref, b_ref):
    mixed = jnp.concatenate([a * (ga * _sigmoid(ga)), p * (gb * _sigmoid(gb))], axis=-1)
    h = DEEPNORM_ALPHA * x + _dot(mixed.astype(BF16), wo_ref[...])
    mu = jnp.mean(h, axis=-1, keepdims=True)
    d = h - mu
    var = jnp.mean(d * d, axis=-1, keepdims=True)
    return d * lax.rsqrt(var + LN_EPS) * g_ref[...] + b_ref[...]


def _combine_kernel(blocks_per_seq, a_ref, ga_ref, gb_ref, u_ref, uh_ref, x_ref,
                    wp_ref, ps_ref, wo_ref, g_ref, b_ref, y_ref, ext_s):
    i = pl.program_id(0)
    tm = u_ref.shape[0]
    first = (i % blocks_per_seq) == 0
    ext_s[0:HALO, :] = jnp.where(first, 0.0, uh_ref[...])
    ext_s[HALO:HALO + tm, :] = u_ref[...]
    pos = (i % blocks_per_seq) * tm + lax.broadcasted_iota(I32, (tm, 1), 0)

    parts = []
    for g, w in enumerate(POOL_WINDOWS):
        cols = slice(g * POOL_GROUP_DIM, (g + 1) * POOL_GROUP_DIM)
        win = ext_s[HALO:HALO + tm, cols]
        for j in range(1, w):
            win = win + ext_s[HALO - j:HALO - j + tm, cols]
        count = jnp.minimum(pos + 1, w).astype(F32)
        pooled = win / count - u_ref[:, cols]
        parts.append(_dot(pooled.astype(BF16), wp_ref[g]))
    p = jnp.concatenate(parts, axis=-1) * ps_ref[...]
    y_ref[...] = _gate_out_norm(a_ref[...], ga_ref[...], p, gb_ref[...], x_ref[...],
                                wo_ref, g_ref, b_ref)


def _combine(a, ga, gb, u, x2d, w_pool_b, pool_scale, w_out_b, ln_g, ln_b, *, seq, tm):
    rows = a.shape[0]
    nb = rows // tm
    bps = seq // tm
    hpb = tm // HALO
    row_blk = lambda n: pl.BlockSpec((tm, n), lambda i: (i, 0))
    const = lambda shape: pl.BlockSpec(shape, lambda i: (0,) * len(shape))
    return pl.pallas_call(
        functools.partial(_combine_kernel, bps),
        grid=(nb,),
        in_specs=[row_blk(ATTN_WIDTH), row_blk(ATTN_WIDTH), row_blk(POOL_WIDTH), row_blk(POOL_WIDTH),
                  pl.BlockSpec((HALO, POOL_WIDTH), lambda i: (jnp.maximum(i * hpb - 1, 0), 0)),
                  row_blk(D_MODEL),
                  const((N_POOL_GROUPS, POOL_GROUP_DIM, POOL_GROUP_DIM)), const((1, POOL_WIDTH)),
                  const((D_MODEL, D_MODEL)), const((1, D_MODEL)), const((1, D_MODEL))],
        out_specs=row_blk(D_MODEL),
        out_shape=jax.ShapeDtypeStruct((rows, D_MODEL), F32),
        scratch_shapes=[pltpu.VMEM((HALO + tm, POOL_WIDTH), F32)],
        compiler_params=pltpu.CompilerParams(dimension_semantics=("arbitrary",),
                                             vmem_limit_bytes=VMEM_LIMIT),
        name="combine",
    )(a, ga, gb, u, u, x2d, w_pool_b, pool_scale, w_out_b, ln_g, ln_b)


def _sample_scores_kernel(n_pages, t_new, pt_ref, qi_ref, w_ref, kn_ref, *refs):
    page_refs, sc_ref = refs[:n_pages], refs[n_pages]
    qi = qi_ref[0]
    w = w_ref[0]

    def head_sum(s):
        r = jnp.maximum(s, 0.0) * w
        return jnp.sum(r.reshape(t_new, IDX_HEADS, s.shape[-1]), axis=1)

    for p in range(n_pages):
        kp_t = page_refs[p][0].astype(BF16)
        sc_ref[0, :, p * PAGE_SIZE:(p + 1) * PAGE_SIZE] = head_sum(_dot(qi, kp_t))
    s_new = head_sum(_dot_nt(qi, kn_ref[0]))
    t_idx = lax.broadcasted_iota(I32, s_new.shape, 0)
    j_idx = lax.broadcasted_iota(I32, s_new.shape, 1)
    sc_ref[0, :, n_pages * PAGE_SIZE:] = jnp.where(j_idx <= t_idx, s_new, NEG_INF)


def _sample_scores(page_table, qi_rows, w_col, ki_new_pad, cache_kidx_t, *, t_new):
    nb, n_pages = page_table.shape
    width = (n_pages + 1) * PAGE_SIZE
    page_spec = lambda p: pl.BlockSpec((1, IDX_DIM, PAGE_SIZE), lambda b, pt, p=p: (pt[b, p], 0, 0))
    grid_spec = pltpu.PrefetchScalarGridSpec(
        num_scalar_prefetch=1,
        grid=(nb,),
        in_specs=[pl.BlockSpec((1, t_new * IDX_HEADS, IDX_DIM), lambda b, pt: (b, 0, 0)),
                  pl.BlockSpec((1, t_new * IDX_HEADS, 1), lambda b, pt: (b, 0, 0)),
                  pl.BlockSpec((1, PAGE_SIZE, IDX_DIM), lambda b, pt: (b, 0, 0))]
                 + [page_spec(p) for p in range(n_pages)],
        out_specs=pl.BlockSpec((1, t_new, width), lambda b, pt: (b, 0, 0)),
    )
    return pl.pallas_call(
        functools.partial(_sample_scores_kernel, n_pages, t_new),
        grid_spec=grid_spec,
        out_shape=jax.ShapeDtypeStruct((nb, t_new, width), F32),
        compiler_params=pltpu.CompilerParams(dimension_semantics=("arbitrary",),
                                             vmem_limit_bytes=VMEM_LIMIT),
        name="sample_scores",
    )(page_table, qi_rows, w_col, ki_new_pad, *([cache_kidx_t] * n_pages))


def _sample_select_kernel(topk, sc_ref, sel_ref):
    def count_ge(cand):
        return jnp.sum((sc_ref[...] >= cand).astype(F32), axis=1, keepdims=True)

    thr = _kth_largest(count_ge, topk, (sc_ref.shape[0], 1))
    sc = sc_ref[...]
    sel_ref[...] = jnp.where((sc >= thr) & (sc > NEG_INF), 1.0, 0.0).astype(F32)


def _sample_select(scores2d, *, topk):
    return pl.pallas_call(
        functools.partial(_sample_select_kernel, topk),
        out_shape=jax.ShapeDtypeStruct(scores2d.shape, F32),
        compiler_params=pltpu.CompilerParams(vmem_limit_bytes=VMEM_LIMIT),
        name="sample_select",
    )(scores2d)


def _expand_matrix():
    e = np.zeros((2 * PAGE_SIZE, PAGE_SIZE * N_HEADS), np.float32)
    c = np.arange(PAGE_SIZE * N_HEADS)
    e[c // N_HEADS, c] = 1.0
    e[PAGE_SIZE + c % N_HEADS, c] = 1.0
    return jnp.asarray(e, BF16)


def _sample_attn_kernel(n_steps, g_pages, t_new, pt_ref, q_ref, e_ref, *refs):
    k_refs, v_refs = refs[:g_pages], refs[g_pages:2 * g_pages]
    sel_ref, kn_ref, vn_ref, seln_ref, a_ref, m_s, l_s, acc_s = refs[2 * g_pages:]
    step = pl.program_id(1)
    rows = t_new * N_HEADS
    cols = PAGE_SIZE * N_HEADS

    @pl.when(step == 0)
    def _():
        m_s[...] = jnp.full(m_s.shape, M_INIT, F32)
        l_s[...] = jnp.zeros(l_s.shape, F32)
        acc_s[...] = jnp.zeros(acc_s.shape, F32)

    q = q_ref[0]
    head_row = lax.broadcasted_iota(I32, (rows, LANES), 0) % N_HEADS
    head_onehot = (lax.broadcasted_iota(I32, (rows, LANES), 1) == head_row).astype(BF16)

    def attend(ks, vs, sels):
        n = len(ks)
        marks = []
        for sel in sels:
            per_row = jnp.concatenate(
                [jnp.broadcast_to(sel[t:t + 1, :], (N_HEADS, PAGE_SIZE)) for t in range(t_new)], axis=0)
            marks.append(jnp.concatenate([per_row.astype(BF16), head_onehot], axis=1))
        valid = _dot(jnp.concatenate(marks, axis=0), e_ref[...]) > 1.5
        s = jnp.concatenate(
            [jnp.where(valid[g * rows:(g + 1) * rows], _dot_nt(q, ks[g]), NEG_INF) for g in range(n)],
            axis=1)
        m_old = m_s[...]
        m_new = jnp.maximum(m_old, jnp.max(s, axis=1, keepdims=True))
        alpha = jnp.exp(m_old - m_new)
        pr = jnp.exp(s - m_new)
        l_s[...] = alpha * l_s[...] + jnp.sum(pr, axis=1, keepdims=True)
        pv = _dot(pr[:, :cols].astype(BF16), vs[0])
        for g in range(1, n):
            pv = pv + _dot(pr[:, g * cols:(g + 1) * cols].astype(BF16), vs[g])
        acc_s[...] = alpha * acc_s[...] + pv
        m_s[...] = m_new

    attend([r[0].astype(BF16) for r in k_refs], [r[0].astype(BF16) for r in v_refs],
           [sel_ref[0, :, g * PAGE_SIZE:(g + 1) * PAGE_SIZE] for g in range(g_pages)])

    @pl.when(step == n_steps - 1)
    def _():
        attend([kn_ref[0]], [vn_ref[0]], [seln_ref[0]])
        a_ref[0] = acc_s[...] / l_s[...]


def _sample_attention(page_table, q_rows, sel_pad, k_new_pad, v_new_pad, cache_k2, cache_v2, *, t_new):
    nb, n_pages = page_table.shape
    g_pages = PAGES_PER_STEP
    n_steps = n_pages // g_pages
    rows = t_new * N_HEADS
    cols = PAGE_SIZE * N_HEADS
    page = lambda g: pl.BlockSpec((1, cols, HEAD_DIM), lambda b, s, pt, g=g: (pt[b, s * g_pages + g], 0, 0))
    per_b = lambda r, n: pl.BlockSpec((1, r, n), lambda b, s, pt: (b, 0, 0))
    grid_spec = pltpu.PrefetchScalarGridSpec(
        num_scalar_prefetch=1,
        grid=(nb, n_steps),
        in_specs=[per_b(rows, HEAD_DIM),
                  pl.BlockSpec((2 * PAGE_SIZE, cols), lambda b, s, pt: (0, 0))]
                 + [page(g) for g in range(g_pages)] + [page(g) for g in range(g_pages)]
                 + [pl.BlockSpec((1, SUBLANES, g_pages * PAGE_SIZE), lambda b, s, pt: (b, 0, s)),
                    per_b(cols, HEAD_DIM), per_b(cols, HEAD_DIM),
                    pl.BlockSpec((1, SUBLANES, PAGE_SIZE), lambda b, s, pt: (b, 0, n_pages))],
        out_specs=per_b(rows, HEAD_DIM),
        scratch_shapes=[pltpu.VMEM((rows, 1), F32), pltpu.VMEM((rows, 1), F32),
                        pltpu.VMEM((rows, HEAD_DIM), F32)],
    )
    return pl.pallas_call(
        functools.partial(_sample_attn_kernel, n_steps, g_pages, t_new),
        grid_spec=grid_spec,
        out_shape=jax.ShapeDtypeStruct((nb, rows, HEAD_DIM), F32),
        compiler_params=pltpu.CompilerParams(dimension_semantics=("arbitrary", "arbitrary"),
                                             vmem_limit_bytes=VMEM_LIMIT),
        name="sample_attn",
    )(page_table, q_rows, _expand_matrix(), *([cache_k2] * g_pages), *([cache_v2] * g_pages),
      sel_pad, k_new_pad, v_new_pad, sel_pad)


def _sample_combine_kernel(t_new, a_ref, ga_ref, gb_ref, ext_ref, x_ref,
                           wp_ref, ps_ref, wo_ref, g_ref, b_ref, y_ref):
    p_t = []
    for t in range(t_new):
        parts = []
        for g, w in enumerate(POOL_WINDOWS):
            cols = slice(g * POOL_GROUP_DIM, (g + 1) * POOL_GROUP_DIM)
            cur = POOL_STATE + t
            win = ext_ref[cur, :, cols]
            for j in range(1, w):
                win = win + ext_ref[cur - j, :, cols]
            pooled = win / float(w) - ext_ref[cur, :, cols]
            parts.append(_dot(pooled.astype(BF16), wp_ref[g]))
        p_t.append(jnp.concatenate(parts, axis=-1) * ps_ref[...])
    y_ref[...] = _gate_out_norm(a_ref[...], ga_ref[...], jnp.concatenate(p_t, axis=0), gb_ref[...],
                                x_ref[...], wo_ref, g_ref, b_ref)


def _sample_combine(a, ga, gb, ext_tm, x2d, w_pool_b, pool_scale, w_out_b, ln_g, ln_b, *, t_new):
    return pl.pallas_call(
        functools.partial(_sample_combine_kernel, t_new),
        out_shape=jax.ShapeDtypeStruct(x2d.shape, F32),
        compiler_params=pltpu.CompilerParams(vmem_limit_bytes=VMEM_LIMIT),
        name="sample_combine",
    )(a, ga, gb, ext_tm, x2d, w_pool_b, pool_scale, w_out_b, ln_g, ln_b)


def kernel(x_prompt, x_sample, cache_k, cache_v, cache_kidx, state_pool, page_table,
           w_in, w_pool, pool_scale, w_out, ln_g, ln_b):
    assert w_in.shape[0] == DEPTH == 1
    batch, seq, _ = x_prompt.shape
    nb, t_new, _ = x_sample.shape
    n_pool, n_pages = cache_k.shape[1], page_table.shape[1]
    past = n_pages * PAGE_SIZE

    w_t = jnp.swapaxes(w_in[0], 0, 1).astype(BF16)
    w_pool_b = w_pool[0].astype(BF16)
    w_out_b = w_out[0].astype(BF16)
    ps, g, b = pool_scale[0][None, :], ln_g[0][None, :], ln_b[0][None, :]

    xp2d = x_prompt.reshape(batch * seq, D_MODEL)
    (q, kf, kb, vf, vb, ga, qi_hm, _, kib, kit, wit, u, gb) = _project(
        xp2d, w_t, jnp.arange(seq), tm=KEY_CHUNK, rows_per_seq=seq, head_major=True)
    a = _prompt_attention(q, qi_hm, wit, kib, kb, vb, batch=batch, seq=seq)
    y_prompt = _combine(a, ga, gb, u, xp2d, w_pool_b, ps, w_out_b, g, b, seq=seq, tm=256)

    rows_s = nb * t_new
    xs2d = x_sample.reshape(rows_s, D_MODEL)
    (qs, kfs, kbs, vfs, vbs, gas, qis, kifs, kibs, _, wits, us, gbs) = _project(
        xs2d, w_t, jnp.tile(past + jnp.arange(t_new), nb), tm=rows_s, rows_per_seq=rows_s,
        head_major=False)

    per_seq = lambda z, r, n: z.reshape(nb, r, n)
    pad_rows = lambda z, n: jnp.pad(z, ((0, 0), (0, n - z.shape[1]), (0, 0)))
    scores = _sample_scores(
        page_table,
        per_seq(qis, t_new * IDX_HEADS, IDX_DIM),
        per_seq(wits.T, t_new * IDX_HEADS, 1),
        pad_rows(per_seq(kibs, t_new, IDX_DIM), PAGE_SIZE),
        jnp.swapaxes(cache_kidx[0], 1, 2),
        t_new=t_new)
    width = scores.shape[-1]
    sel = _sample_select(scores.reshape(rows_s, width), topk=min(TOPK_MAX, (past + t_new) // 4))
    head_rows = t_new * N_HEADS
    a_s = _sample_attention(
        page_table,
        per_seq(qs, head_rows, HEAD_DIM),
        pad_rows(per_seq(sel, t_new, width), SUBLANES),
        pad_rows(per_seq(kbs, head_rows, HEAD_DIM), PAGE_SIZE * N_HEADS),
        pad_rows(per_seq(vbs, head_rows, HEAD_DIM), PAGE_SIZE * N_HEADS),
        cache_k.reshape(n_pool, PAGE_SIZE * N_HEADS, HEAD_DIM),
        cache_v.reshape(n_pool, PAGE_SIZE * N_HEADS, HEAD_DIM),
        t_new=t_new).reshape(rows_s, ATTN_WIDTH)

    u_ext = jnp.concatenate([state_pool[0], us.reshape(nb, t_new, POOL_WIDTH)], axis=1)
    time_major = lambda z: z.reshape(nb, t_new, z.shape[-1]).transpose(1, 0, 2).reshape(rows_s, z.shape[-1])
    y_tm = _sample_combine(time_major(a_s), time_major(gas), time_major(gbs), u_ext.transpose(1, 0, 2),
                           time_major(xs2d), w_pool_b, ps, w_out_b, g, b, t_new=t_new)
    y_sample = y_tm.reshape(t_new, nb, D_MODEL).transpose(1, 0, 2)

    hd = (N_HEADS, HEAD_DIM)
    return (y_prompt.reshape(batch, seq, D_MODEL),
            y_sample,
            kf.reshape(1, batch, seq, *hd), vf.reshape(1, batch, seq, *hd),
            jnp.swapaxes(kit, 1, 2)[None],
            u.reshape(batch, seq, POOL_WIDTH)[None, :, -POOL_STATE:],
            kfs.reshape(1, nb, t_new, *hd), vfs.reshape(1, nb, t_new, *hd),
            kifs.reshape(1, nb, t_new, IDX_DIM),
            u_ext[None, :, -POOL_STATE:])
```

```python
import functools

import numpy as np
import jax
import jax.numpy as jnp
from jax import lax
from jax.experimental import pallas as pl
from jax.experimental.pallas import tpu as pltpu

F32 = jnp.float32
BF16 = jnp.bfloat16
I32 = jnp.int32

D_MODEL = 2048
PAGE_SIZE = 128
N_HEADS = 8
HEAD_DIM = 128
ATTN_WIDTH = N_HEADS * HEAD_DIM
ROT_DIM = HEAD_DIM // 4
ROPE_THETA = 500000.0
IDX_HEADS = 16
IDX_DIM = 64
IDX_ROT_DIM = IDX_DIM // 4
TOPK_MAX = 256
POOL_WINDOWS = (2, 4, 8, 16)
N_POOL_GROUPS = len(POOL_WINDOWS)
POOL_WIDTH = D_MODEL - ATTN_WIDTH
POOL_GROUP_DIM = POOL_WIDTH // N_POOL_GROUPS
POOL_STATE = max(POOL_WINDOWS) - 1
IN_SPLITS = (ATTN_WIDTH, ATTN_WIDTH, ATTN_WIDTH, ATTN_WIDTH,
             IDX_HEADS * IDX_DIM, IDX_DIM, IDX_HEADS, POOL_WIDTH, POOL_WIDTH)
IN_COLS = sum(IN_SPLITS)
DEPTH = 1
DEEPNORM_ALPHA = (2 * DEPTH) ** 0.25
LN_EPS = 1e-5
ATTN_SCALE = HEAD_DIM ** -0.5
IDX_SCALE = (IDX_DIM ** -0.5) * (IDX_HEADS ** -0.5)

LANES = 128
SUBLANES = 8
VMEM_LIMIT = 56 * 1024 * 1024

(_OFF_Q, _OFF_K, _OFF_V, _OFF_GA, _OFF_QI, _OFF_KI, _OFF_WI, _OFF_U, _OFF_GB) = (
    [0] + np.cumsum(IN_SPLITS)[:-1].tolist())

NEG_INF = float("-inf")
M_INIT = -1e30
INT_MIN = -2 ** 31
KEY_OF_NEG_INF = -2139095041

KEY_CHUNK = 256
HALF_CHUNK = 128
Q_TILE = 256
PAGES_PER_STEP = 16


def _dot(a, b):
    return jnp.dot(a, b, preferred_element_type=F32)


def _dot_nt(a, b):
    return lax.dot_general(a, b, (((1,), (1,)), ((), ())), preferred_element_type=F32)


def _sigmoid(x):
    return 1.0 / (1.0 + jnp.exp(-x))


def _rope_tables_t(pos, rot):
    half = rot // 2
    freqs = ROPE_THETA ** (-jnp.arange(half, dtype=F32) * (2.0 / rot))
    ang = pos.astype(F32)[:, None] * freqs[None, :]
    return jnp.stack([jnp.cos(ang).T, jnp.sin(ang).T]).astype(F32)


def _rope_tables(pos, rot, group):
    half = rot // 2
    freqs = ROPE_THETA ** (-jnp.arange(half, dtype=F32) * (2.0 / rot))
    ang = pos.astype(F32)[:, None] * freqs[None, :]
    cos, sin = jnp.cos(ang), jnp.sin(ang)
    lane = np.arange(LANES) % group
    sel = lane % half
    cos_l, sin_l = cos[:, sel], sin[:, sel]
    first = jnp.asarray(lane < half)[None, :]
    second = jnp.asarray((lane >= half) & (lane < rot))[None, :]
    c = jnp.where(first | second, cos_l, 1.0)
    s1 = jnp.where(first, -sin_l, 0.0)
    s2 = jnp.where(second, sin_l, 0.0)
    return jnp.stack([c, s1, s2]).astype(F32)


def _rope(x, tab_ref, half):
    fwd = pltpu.roll(x, LANES - half, 1)
    bwd = pltpu.roll(x, half, 1)
    return x * tab_ref[0] + fwd * tab_ref[1] + bwd * tab_ref[2]


def _proj_kernel(head_major, x_ref, wt_ref, th_ref, ti_ref, tit_ref,
                 q_ref, kf_ref, kb_ref, vf_ref, vb_ref, ga_ref, qi_ref, kif_ref, kib_ref, kit_ref,
                 wit_ref, u_ref, gb_ref):
    xb = x_ref[...].astype(BF16)

    def seg(off, n):
        return _dot_nt(xb, wt_ref[off:off + n, :])

    def seg_t(off, n):
        return _dot_nt(wt_ref[off:off + n, :], xb)

    qf, kf = seg(_OFF_Q, ATTN_WIDTH), seg(_OFF_K, ATTN_WIDTH)
    for h in range(N_HEADS):
        sl = slice(h * HEAD_DIM, (h + 1) * HEAD_DIM)
        qh = _rope(qf[:, sl], th_ref, ROT_DIM // 2)
        q_ref[:, sl] = (qh * ATTN_SCALE).astype(BF16)
        kh = _rope(kf[:, sl], th_ref, ROT_DIM // 2)
        kf_ref[:, sl] = kh
        kb_ref[:, sl] = kh.astype(BF16)

    v = seg(_OFF_V, ATTN_WIDTH)
    vf_ref[...] = v
    if head_major:
        vb_ref[0] = v.T.astype(BF16)
    else:
        vb_ref[...] = v.astype(BF16)
    ga_ref[...] = seg(_OFF_GA, ATTN_WIDTH)
    u_ref[...] = seg(_OFF_U, POOL_WIDTH)
    gb_ref[...] = seg(_OFF_GB, POOL_WIDTH)

    qif = seg(_OFF_QI, IDX_HEADS * IDX_DIM)
    for j in range(IDX_HEADS // 2):
        r = _rope(qif[:, j * LANES:(j + 1) * LANES], ti_ref, IDX_ROT_DIM // 2).astype(BF16)
        if head_major:
            qi_ref[2 * j] = r[:, :IDX_DIM]
            qi_ref[2 * j + 1] = r[:, IDX_DIM:]
        else:
            qi_ref[:, j * LANES:(j + 1) * LANES] = r

    half = IDX_ROT_DIM // 2
    kt = seg_t(_OFF_KI, IDX_DIM)
    x1, x2 = kt[:half], kt[half:2 * half]
    cos_t, sin_t = tit_ref[0], tit_ref[1]
    kit_ref[0] = jnp.concatenate([x1 * cos_t - x2 * sin_t, x1 * sin_t + x2 * cos_t, kt[2 * half:]], axis=0)
    kn = jnp.concatenate([seg(_OFF_KI, IDX_DIM), jnp.zeros((xb.shape[0], LANES - IDX_DIM), F32)], axis=1)
    ki = _rope(kn, ti_ref, half)[:, :IDX_DIM]
    kif_ref[...] = ki
    kib_ref[...] = ki.astype(BF16)

    wit_ref[...] = seg_t(_OFF_WI, IDX_HEADS) * IDX_SCALE


def _project(x2d, w_t, pos, *, tm, rows_per_seq, head_major):
    rows = x2d.shape[0]
    nb = rows // tm
    pos_blocks = pos.shape[0] // tm
    seq_blocks = rows_per_seq // tm
    tab_h = _rope_tables(pos, ROT_DIM, HEAD_DIM)
    tab_i = _rope_tables(pos, IDX_ROT_DIM, IDX_DIM)
    tab_it = _rope_tables_t(pos, IDX_ROT_DIM)

    row_blk = lambda n: pl.BlockSpec((tm, n), lambda i: (i, 0))
    tab_blk = pl.BlockSpec((3, tm, LANES), lambda i: (0, i % pos_blocks, 0))
    tabt_blk = pl.BlockSpec((2, IDX_ROT_DIM // 2, tm), lambda i: (0, 0, i % pos_blocks))
    wide = lambda dt: jax.ShapeDtypeStruct((rows, ATTN_WIDTH), dt)
    if head_major:
        qi_shape = jax.ShapeDtypeStruct((IDX_HEADS, rows, IDX_DIM), BF16)
        qi_blk = pl.BlockSpec((IDX_HEADS, tm, IDX_DIM), lambda i: (0, i, 0))
        vb_shape = jax.ShapeDtypeStruct((nb, ATTN_WIDTH, tm), BF16)
        vb_blk = pl.BlockSpec((1, ATTN_WIDTH, tm), lambda i: (i, 0, 0))
    else:
        qi_shape = jax.ShapeDtypeStruct((rows, IDX_HEADS * IDX_DIM), BF16)
        qi_blk = row_blk(IDX_HEADS * IDX_DIM)
        vb_shape, vb_blk = wide(BF16), row_blk(ATTN_WIDTH)

    out_shape = (wide(BF16), wide(F32), wide(BF16), wide(F32), vb_shape, wide(F32),
                 qi_shape,
                 jax.ShapeDtypeStruct((rows, IDX_DIM), F32), jax.ShapeDtypeStruct((rows, IDX_DIM), BF16),
                 jax.ShapeDtypeStruct((rows // rows_per_seq, IDX_DIM, rows_per_seq), F32),
                 jax.ShapeDtypeStruct((IDX_HEADS, rows), F32), wide(F32), wide(F32))
    out_specs = (row_blk(ATTN_WIDTH),) * 4 + (vb_blk, row_blk(ATTN_WIDTH)) + (
        qi_blk, row_blk(IDX_DIM), row_blk(IDX_DIM),
        pl.BlockSpec((1, IDX_DIM, tm), lambda i: (i // seq_blocks, 0, i % seq_blocks)),
        pl.BlockSpec((IDX_HEADS, tm), lambda i: (0, i)),
        row_blk(POOL_WIDTH), row_blk(POOL_WIDTH))
    return pl.pallas_call(
        functools.partial(_proj_kernel, head_major),
        grid=(nb,),
        in_specs=[row_blk(D_MODEL),
                  pl.BlockSpec((IN_COLS, D_MODEL), lambda i: (0, 0), pipeline_mode=pl.Buffered(1)),
                  tab_blk, tab_blk, tabt_blk],
        out_specs=out_specs,
        out_shape=out_shape,
        compiler_params=pltpu.CompilerParams(dimension_semantics=("arbitrary",),
                                             vmem_limit_bytes=VMEM_LIMIT),
        name="proj_hm" if head_major else "proj_rm",
    )(x2d, w_t, tab_h, tab_i, tab_it)


def _key_to_float(key):
    return pltpu.bitcast(key ^ ((key >> 31) & 0x7FFFFFFF), F32)


def _kth_largest(count_ge, topk, shape):
    def bit_body(it, prefix):
        trial = prefix | jnp.left_shift(jnp.int32(1), 31 - it)
        cnt = count_ge(_key_to_float(trial ^ INT_MIN))
        return jnp.where(cnt >= topk, trial, prefix)

    prefix = lax.fori_loop(0, 32, bit_body, jnp.zeros(shape, I32))
    return _key_to_float(jnp.maximum(prefix ^ INT_MIN, KEY_OF_NEG_INF))


def _fold_rows(x, op):
    return op(x.reshape(x.shape[0] // SUBLANES, SUBLANES, x.shape[1]), axis=0)


def _prompt_attn_kernel(topk, q_ref, qi_ref, wit_ref, ki_ref, k_ref, vt_ref, a_ref,
                        sc_s, bias_s, acc_s):
    i = pl.program_id(1)
    n_chunks = (i * Q_TILE + Q_TILE + KEY_CHUNK - 1) // KEY_CHUNK
    qpos = i * Q_TILE + lax.broadcasted_iota(I32, (1, Q_TILE), 1)
    row = lax.broadcasted_iota(I32, (HALF_CHUNK, 1), 0)
    halves = [slice(r * HALF_CHUNK, (r + 1) * HALF_CHUNK) for r in range(KEY_CHUNK // HALF_CHUNK)]

    def score_chunk(c, carry):
        for rows in halves:
            kic = ki_ref[c, rows, :]
            acc = jnp.zeros((HALF_CHUNK, Q_TILE), F32)
            for h in range(IDX_HEADS):
                st = _dot_nt(kic, qi_ref[h])
                acc = acc + jnp.maximum(st, 0.0) * wit_ref[h:h + 1, :]
            causal = (c * KEY_CHUNK + rows.start + row) <= qpos
            sc_s[c, rows, :] = jnp.where(causal, acc, NEG_INF)
        return carry

    lax.fori_loop(0, n_chunks, score_chunk, 0)

    def count(pred):
        def body(c, cnt):
            for rows in halves:
                cnt = cnt + _fold_rows(pred(sc_s[c, rows, :]).astype(F32), jnp.sum)
            return cnt
        cnt = lax.fori_loop(0, n_chunks, body, jnp.zeros((SUBLANES, Q_TILE), F32))
        return jnp.sum(cnt, axis=0, keepdims=True)

    thr = _kth_largest(lambda cand: count(lambda x: x >= cand), topk, (1, Q_TILE))

    surplus = (count(lambda x: x >= thr) > topk) & (thr > NEG_INF)
    has_ties = jnp.max(surplus.astype(F32)) > 0.5

    @pl.when(jnp.logical_not(has_ties))
    def _():
        def bias_chunk(c, carry):
            for rows in halves:
                sel = (sc_s[c, rows, :] >= thr) & ((c * KEY_CHUNK + rows.start + row) <= qpos)
                bias_s[c, rows, :] = jnp.where(sel, 0.0, NEG_INF).astype(F32)
            return carry

        lax.fori_loop(0, n_chunks, bias_chunk, 0)

    @pl.when(has_ties)
    def _():
        quota = topk - count(lambda x: x > thr)
        earlier = (lax.broadcasted_iota(I32, (HALF_CHUNK, HALF_CHUNK), 0)
                   > lax.broadcasted_iota(I32, (HALF_CHUNK, HALF_CHUNK), 1)).astype(BF16)

        def bias_chunk(c, seen):
            for rows in halves:
                x = sc_s[c, rows, :]
                tie = (x == thr).astype(BF16)
                before = seen + _dot(earlier, tie)
                sel = ((x > thr) | ((x == thr) & (before < quota))) \
                    & ((c * KEY_CHUNK + rows.start + row) <= qpos)
                bias_s[c, rows, :] = jnp.where(sel, 0.0, NEG_INF).astype(F32)
                seen = seen + jnp.sum(tie.astype(F32), axis=0, keepdims=True)
            return seen

        lax.fori_loop(0, n_chunks, bias_chunk, jnp.zeros((1, Q_TILE), F32))

    acc_s[...] = jnp.zeros(acc_s.shape, F32)
    lane_groups = [slice(g * LANES, (g + 1) * LANES) for g in range(Q_TILE // LANES)]
    n_state = N_HEADS * len(lane_groups)

    def attn_chunk(c, state):
        ms, ls = state
        new_ms, new_ls = [], []
        for h in range(N_HEADS):
            sl = slice(h * HEAD_DIM, (h + 1) * HEAD_DIM)
            for g, qs in enumerate(lane_groups):
                m_old, l_old = ms[h * len(lane_groups) + g], ls[h * len(lane_groups) + g]
                s = _dot_nt(k_ref[c, :, sl], q_ref[qs, sl]) + bias_s[c, :, qs]
                m_new = jnp.maximum(m_old, jnp.max(_fold_rows(s, jnp.max), axis=0, keepdims=True))
                alpha = jnp.exp(m_old - m_new)
                p = jnp.exp(s - m_new)
                new_ms.append(m_new)
                new_ls.append(alpha * l_old + _fold_rows(p, jnp.sum))
                acc_s[h, :, qs] = alpha * acc_s[h, :, qs] + _dot(vt_ref[c, sl, :], p.astype(BF16))
        return tuple(new_ms), tuple(new_ls)

    _, ls = lax.fori_loop(
        0, n_chunks, attn_chunk,
        (tuple(jnp.full((1, LANES), M_INIT, F32) for _ in range(n_state)),
         tuple(jnp.zeros((SUBLANES, LANES), F32) for _ in range(n_state))))

    for h in range(N_HEADS):
        l = jnp.concatenate([jnp.sum(ls[h * len(lane_groups) + g], axis=0, keepdims=True)
                             for g in range(len(lane_groups))], axis=1)
        a_ref[:, h * HEAD_DIM:(h + 1) * HEAD_DIM] = (acc_s[h] / l).T


def _prompt_attention(q, qi_hm, wit, ki_b, k_b, vt3, *, batch, seq):
    nq = seq // Q_TILE
    nch = seq // KEY_CHUNK
    topk = min(TOPK_MAX, seq // 4)
    ki3 = ki_b.reshape(batch * nch, KEY_CHUNK, IDX_DIM)
    k3 = k_b.reshape(batch * nch, KEY_CHUNK, ATTN_WIDTH)
    resident = lambda r, n: pl.BlockSpec((nch, r, n), lambda b, i: (b, 0, 0),
                                         pipeline_mode=pl.Buffered(1))
    return pl.pallas_call(
        functools.partial(_prompt_attn_kernel, topk),
        grid=(batch, nq),
        in_specs=[pl.BlockSpec((Q_TILE, ATTN_WIDTH), lambda b, i: (b * nq + i, 0)),
                  pl.BlockSpec((IDX_HEADS, Q_TILE, IDX_DIM), lambda b, i: (0, b * nq + i, 0)),
                  pl.BlockSpec((IDX_HEADS, Q_TILE), lambda b, i: (0, b * nq + i)),
                  resident(KEY_CHUNK, IDX_DIM), resident(KEY_CHUNK, ATTN_WIDTH),
                  resident(ATTN_WIDTH, KEY_CHUNK)],
        out_specs=pl.BlockSpec((Q_TILE, ATTN_WIDTH), lambda b, i: (b * nq + i, 0)),
        out_shape=jax.ShapeDtypeStruct((batch * seq, ATTN_WIDTH), F32),
        scratch_shapes=[pltpu.VMEM((nch, KEY_CHUNK, Q_TILE), F32),
                        pltpu.VMEM((nch, KEY_CHUNK, Q_TILE), F32),
                        pltpu.VMEM((N_HEADS, HEAD_DIM, Q_TILE), F32)],
        compiler_params=pltpu.CompilerParams(dimension_semantics=("arbitrary", "arbitrary"),
                                             vmem_limit_bytes=VMEM_LIMIT),
        name="prompt_attn",
    )(q, qi_hm, wit, ki3, k3, vt3)


HALO = 16


def _gate_out_norm(a, ga, p, gb, x, wo_ref, g_ref, b_ref):
    mixed = jnp.concatenate([a * (ga * _sigmoid(ga)), p * (gb * _sigmoid(gb))], axis=-1)
    h = DEEPNORM_ALPHA * x + _dot(mixed.astype(BF16), wo_ref[...])
    mu = jnp.mean(h, axis=-1, keepdims=True)
    d = h - mu
    var = jnp.mean(d * d, axis=-1, keepdims=True)
    return d * lax.rsqrt(var + LN_EPS) * g_ref[...] + b_ref[...]


def _combine_kernel(blocks_per_seq, a_ref, ga_ref, gb_ref, u_ref, uh_ref, x_ref,
                    wp_ref, ps_ref, wo_ref, g_ref, b_ref, y_ref, ext_s):
    i = pl.program_id(0)
    tm = u_ref.shape[0]
    first = (i % blocks_per_seq) == 0
    ext_s[0:HALO, :] = jnp.where(first, 0.0, uh_ref[...])
    ext_s[HALO:HALO + tm, :] = u_ref[...]
    pos = (i % blocks_per_seq) * tm + lax.broadcasted_iota(I32, (tm, 1), 0)

    parts = []
    for g, w in enumerate(POOL_WINDOWS):
        cols = slice(g * POOL_GROUP_DIM, (g + 1) * POOL_GROUP_DIM)
        win = ext_s[HALO:HALO + tm, cols]
        for j in range(1, w):
            win = win + ext_s[HALO - j:HALO - j + tm, cols]
        count = jnp.minimum(pos + 1, w).astype(F32)
        pooled = win / count - u_ref[:, cols]
        parts.append(_dot(pooled.astype(BF16), wp_ref[g]))
    p = jnp.concatenate(parts, axis=-1) * ps_ref[...]
    y_ref[...] = _gate_out_norm(a_ref[...], ga_ref[...], p, gb_ref[...], x_ref[...],
                                wo_ref, g_ref, b_ref)


def _combine(a, ga, gb, u, x2d, w_pool_b, pool_scale, w_out_b, ln_g, ln_b, *, seq, tm):
    rows = a.shape[0]
    nb = rows // tm
    bps = seq // tm
    hpb = tm // HALO
    row_blk = lambda n: pl.BlockSpec((tm, n), lambda i: (i, 0))
    const = lambda shape: pl.BlockSpec(shape, lambda i: (0,) * len(shape))
    return pl.pallas_call(
        functools.partial(_combine_kernel, bps),
        grid=(nb,),
        in_specs=[row_blk(ATTN_WIDTH), row_blk(ATTN_WIDTH), row_blk(POOL_WIDTH), row_blk(POOL_WIDTH),
                  pl.BlockSpec((HALO, POOL_WIDTH), lambda i: (jnp.maximum(i * hpb - 1, 0), 0)),
                  row_blk(D_MODEL),
                  const((N_POOL_GROUPS, POOL_GROUP_DIM, POOL_GROUP_DIM)), const((1, POOL_WIDTH)),
                  const((D_MODEL, D_MODEL)), const((1, D_MODEL)), const((1, D_MODEL))],
        out_specs=row_blk(D_MODEL),
        out_shape=jax.ShapeDtypeStruct((rows, D_MODEL), F32),
        scratch_shapes=[pltpu.VMEM((HALO + tm, POOL_WIDTH), F32)],
        compiler_params=pltpu.CompilerParams(dimension_semantics=("arbitrary",),
                                             vmem_limit_bytes=VMEM_LIMIT),
        name="combine",
    )(a, ga, gb, u, u, x2d, w_pool_b, pool_scale, w_out_b, ln_g, ln_b)


def _sample_scores_kernel(n_pages, t_new, pt_ref, qi_ref, w_ref, kn_ref, *refs):
    page_refs, sc_ref = refs[:n_pages], refs[n_pages]
    qi = qi_ref[0]
    w = w_ref[0]

    def head_sum(s):
        r = jnp.maximum(s, 0.0) * w
        return jnp.sum(r.reshape(t_new, IDX_HEADS, s.shape[-1]), axis=1)

    for p in range(n_pages):
        kp_t = page_refs[p][0].astype(BF16)
        sc_ref[0, :, p * PAGE_SIZE:(p + 1) * PAGE_SIZE] = head_sum(_dot(qi, kp_t))
    s_new = head_sum(_dot_nt(qi, kn_ref[0]))
    t_idx = lax.broadcasted_iota(I32, s_new.shape, 0)
    j_idx = lax.broadcasted_iota(I32, s_new.shape, 1)
    sc_ref[0, :, n_pages * PAGE_SIZE:] = jnp.where(j_idx <= t_idx, s_new, NEG_INF)


def _sample_scores(page_table, qi_rows, w_col, ki_new_pad, cache_kidx_t, *, t_new):
    nb, n_pages = page_table.shape
    width = (n_pages + 1) * PAGE_SIZE
    page_spec = lambda p: pl.BlockSpec((1, IDX_DIM, PAGE_SIZE), lambda b, pt, p=p: (pt[b, p], 0, 0))
    grid_spec = pltpu.PrefetchScalarGridSpec(
        num_scalar_prefetch=1,
        grid=(nb,),
        in_specs=[pl.BlockSpec((1, t_new * IDX_HEADS, IDX_DIM), lambda b, pt: (b, 0, 0)),
                  pl.BlockSpec((1, t_new * IDX_HEADS, 1), lambda b, pt: (b, 0, 0)),
                  pl.BlockSpec((1, PAGE_SIZE, IDX_DIM), lambda b, pt: (b, 0, 0))]
                 + [page_spec(p) for p in range(n_pages)],
        out_specs=pl.BlockSpec((1, t_new, width), lambda b, pt: (b, 0, 0)),
    )
    return pl.pallas_call(
        functools.partial(_sample_scores_kernel, n_pages, t_new),
        grid_spec=grid_spec,
        out_shape=jax.ShapeDtypeStruct((nb, t_new, width), F32),
        compiler_params=pltpu.CompilerParams(dimension_semantics=("arbitrary",),
                                             vmem_limit_bytes=VMEM_LIMIT),
        name="sample_scores",
    )(page_table, qi_rows, w_col, ki_new_pad, *([cache_kidx_t] * n_pages))


def _sample_select_kernel(topk, sc_ref, sel_ref):
    rows, width = sc_ref.shape

    def count(pred):
        return jnp.sum(pred(sc_ref[...]).astype(F32), axis=1, keepdims=True)

    thr = _kth_largest(lambda cand: count(lambda x: x >= cand), topk, (rows, 1))

    surplus = (count(lambda x: x >= thr) > topk) & (thr > NEG_INF)
    has_ties = jnp.max(surplus.astype(F32)) > 0.5

    @pl.when(jnp.logical_not(has_ties))
    def _():
        sc = sc_ref[...]
        sel_ref[...] = jnp.where((sc >= thr) & (sc > NEG_INF), 1.0, 0.0).astype(F32)

    @pl.when(has_ties)
    def _():
        quota = topk - count(lambda x: x > thr)
        earlier = (lax.broadcasted_iota(I32, (LANES, LANES), 0)
                   < lax.broadcasted_iota(I32, (LANES, LANES), 1)).astype(BF16)
        seen = jnp.zeros((rows, 1), F32)
        for j in range(width // LANES):
            cols = slice(j * LANES, (j + 1) * LANES)
            x = sc_ref[:, cols]
            tie = (x == thr).astype(BF16)
            before = seen + _dot(tie, earlier)
            sel = ((x > thr) | ((x == thr) & (before < quota))) & (x > NEG_INF)
            sel_ref[:, cols] = jnp.where(sel, 1.0, 0.0).astype(F32)
            seen = seen + jnp.sum(tie.astype(F32), axis=1, keepdims=True)


def _sample_select(scores2d, *, topk):
    return pl.pallas_call(
        functools.partial(_sample_select_kernel, topk),
        out_shape=jax.ShapeDtypeStruct(scores2d.shape, F32),
        compiler_params=pltpu.CompilerParams(vmem_limit_bytes=VMEM_LIMIT),
        name="sample_select",
    )(scores2d)


def _expand_matrix():
    e = np.zeros((2 * PAGE_SIZE, PAGE_SIZE * N_HEADS), np.float32)
    c = np.arange(PAGE_SIZE * N_HEADS)
    e[c // N_HEADS, c] = 1.0
    e[PAGE_SIZE + c % N_HEADS, c] = 1.0
    return jnp.asarray(e, BF16)


def _sample_attn_kernel(n_steps, g_pages, t_new, pt_ref, q_ref, e_ref, *refs):
    k_refs, v_refs = refs[:g_pages], refs[g_pages:2 * g_pages]
    sel_ref, kn_ref, vn_ref, seln_ref, a_ref, m_s, l_s, acc_s = refs[2 * g_pages:]
    step = pl.program_id(1)
    rows = t_new * N_HEADS
    cols = PAGE_SIZE * N_HEADS

    @pl.when(step == 0)
    def _():
        m_s[...] = jnp.full(m_s.shape, M_INIT, F32)
        l_s[...] = jnp.zeros(l_s.shape, F32)
        acc_s[...] = jnp.zeros(acc_s.shape, F32)

    q = q_ref[0]
    head_row = lax.broadcasted_iota(I32, (rows, LANES), 0) % N_HEADS
    head_onehot = (lax.broadcasted_iota(I32, (rows, LANES), 1) == head_row).astype(BF16)

    def attend(ks, vs, sels):
        n = len(ks)
        marks = []
        for sel in sels:
            per_row = jnp.concatenate(
                [jnp.broadcast_to(sel[t:t + 1, :], (N_HEADS, PAGE_SIZE)) for t in range(t_new)], axis=0)
            marks.append(jnp.concatenate([per_row.astype(BF16), head_onehot], axis=1))
        valid = _dot(jnp.concatenate(marks, axis=0), e_ref[...]) > 1.5
        s = jnp.concatenate(
            [jnp.where(valid[g * rows:(g + 1) * rows], _dot_nt(q, ks[g]), NEG_INF) for g in range(n)],
            axis=1)
        m_old = m_s[...]
        m_new = jnp.maximum(m_old, jnp.max(s, axis=1, keepdims=True))
        alpha = jnp.exp(m_old - m_new)
        pr = jnp.exp(s - m_new)
        l_s[...] = alpha * l_s[...] + jnp.sum(pr, axis=1, keepdims=True)
        pv = _dot(pr[:, :cols].astype(BF16), vs[0])
        for g in range(1, n):
            pv = pv + _dot(pr[:, g * cols:(g + 1) * cols].astype(BF16), vs[g])
        acc_s[...] = alpha * acc_s[...] + pv
        m_s[...] = m_new

    attend([r[0].astype(BF16) for r in k_refs], [r[0].astype(BF16) for r in v_refs],
           [sel_ref[0, :, g * PAGE_SIZE:(g + 1) * PAGE_SIZE] for g in range(g_pages)])

    @pl.when(step == n_steps - 1)
    def _():
        attend([kn_ref[0]], [vn_ref[0]], [seln_ref[0]])
        a_ref[0] = acc_s[...] / l_s[...]


def _sample_attention(page_table, q_rows, sel_pad, k_new_pad, v_new_pad, cache_k2, cache_v2, *, t_new):
    nb, n_pages = page_table.shape
    g_pages = PAGES_PER_STEP
    n_steps = n_pages // g_pages
    rows = t_new * N_HEADS
    cols = PAGE_SIZE * N_HEADS
    page = lambda g: pl.BlockSpec((1, cols, HEAD_DIM), lambda b, s, pt, g=g: (pt[b, s * g_pages + g], 0, 0))
    per_b = lambda r, n: pl.BlockSpec((1, r, n), lambda b, s, pt: (b, 0, 0))
    grid_spec = pltpu.PrefetchScalarGridSpec(
        num_scalar_prefetch=1,
        grid=(nb, n_steps),
        in_specs=[per_b(rows, HEAD_DIM),
                  pl.BlockSpec((2 * PAGE_SIZE, cols), lambda b, s, pt: (0, 0))]
                 + [page(g) for g in range(g_pages)] + [page(g) for g in range(g_pages)]
                 + [pl.BlockSpec((1, SUBLANES, g_pages * PAGE_SIZE), lambda b, s, pt: (b, 0, s)),
                    per_b(cols, HEAD_DIM), per_b(cols, HEAD_DIM),
                    pl.BlockSpec((1, SUBLANES, PAGE_SIZE), lambda b, s, pt: (b, 0, n_pages))],
        out_specs=per_b(rows, HEAD_DIM),
        scratch_shapes=[pltpu.VMEM((rows, 1), F32), pltpu.VMEM((rows, 1), F32),
                        pltpu.VMEM((rows, HEAD_DIM), F32)],
    )
    return pl.pallas_call(
        functools.partial(_sample_attn_kernel, n_steps, g_pages, t_new),
        grid_spec=grid_spec,
        out_shape=jax.ShapeDtypeStruct((nb, rows, HEAD_DIM), F32),
        compiler_params=pltpu.CompilerParams(dimension_semantics=("arbitrary", "arbitrary"),
                                             vmem_limit_bytes=VMEM_LIMIT),
        name="sample_attn",
    )(page_table, q_rows, _expand_matrix(), *([cache_k2] * g_pages), *([cache_v2] * g_pages),
      sel_pad, k_new_pad, v_new_pad, sel_pad)


def _sample_combine_kernel(t_new, a_ref, ga_ref, gb_ref, ext_ref, x_ref,
                           wp_ref, ps_ref, wo_ref, g_ref, b_ref, y_ref):
    p_t = []
    for t in range(t_new):
        parts = []
        for g, w in enumerate(POOL_WINDOWS):
            cols = slice(g * POOL_GROUP_DIM, (g + 1) * POOL_GROUP_DIM)
            cur = POOL_STATE + t
            win = ext_ref[cur, :, cols]
            for j in range(1, w):
                win = win + ext_ref[cur - j, :, cols]
            pooled = win / float(w) - ext_ref[cur, :, cols]
            parts.append(_dot(pooled.astype(BF16), wp_ref[g]))
        p_t.append(jnp.concatenate(parts, axis=-1) * ps_ref[...])
    y_ref[...] = _gate_out_norm(a_ref[...], ga_ref[...], jnp.concatenate(p_t, axis=0), gb_ref[...],
                                x_ref[...], wo_ref, g_ref, b_ref)


def _sample_combine(a, ga, gb, ext_tm, x2d, w_pool_b, pool_scale, w_out_b, ln_g, ln_b, *, t_new):
    return pl.pallas_call(
        functools.partial(_sample_combine_kernel, t_new),
        out_shape=jax.ShapeDtypeStruct(x2d.shape, F32),
        compiler_params=pltpu.CompilerParams(vmem_limit_bytes=VMEM_LIMIT),
        name="sample_combine",
    )(a, ga, gb, ext_tm, x2d, w_pool_b, pool_scale, w_out_b, ln_g, ln_b)


def kernel(x_prompt, x_sample, cache_k, cache_v, cache_kidx, state_pool, page_table,
           w_in, w_pool, pool_scale, w_out, ln_g, ln_b):
    assert w_in.shape[0] == DEPTH == 1
    batch, seq, _ = x_prompt.shape
    nb, t_new, _ = x_sample.shape
    n_pool, n_pages = cache_k.shape[1], page_table.shape[1]
    past = n_pages * PAGE_SIZE

    w_t = jnp.swapaxes(w_in[0], 0, 1).astype(BF16)
    w_pool_b = w_pool[0].astype(BF16)
    w_out_b = w_out[0].astype(BF16)
    ps, g, b = pool_scale[0][None, :], ln_g[0][None, :], ln_b[0][None, :]

    xp2d = x_prompt.reshape(batch * seq, D_MODEL)
    (q, kf, kb, vf, vb, ga, qi_hm, _, kib, kit, wit, u, gb) = _project(
        xp2d, w_t, jnp.arange(seq), tm=KEY_CHUNK, rows_per_seq=seq, head_major=True)
    a = _prompt_attention(q, qi_hm, wit, kib, kb, vb, batch=batch, seq=seq)
    y_prompt = _combine(a, ga, gb, u, xp2d, w_pool_b, ps, w_out_b, g, b, seq=seq, tm=256)

    rows_s = nb * t_new
    xs2d = x_sample.reshape(rows_s, D_MODEL)
    (qs, kfs, kbs, vfs, vbs, gas, qis, kifs, kibs, _, wits, us, gbs) = _project(
        xs2d, w_t, jnp.tile(past + jnp.arange(t_new), nb), tm=rows_s, rows_per_seq=rows_s,
        head_major=False)

    per_seq = lambda z, r, n: z.reshape(nb, r, n)
    pad_rows = lambda z, n: jnp.pad(z, ((0, 0), (0, n - z.shape[1]), (0, 0)))
    scores = _sample_scores(
        page_table,
        per_seq(qis, t_new * IDX_HEADS, IDX_DIM),
        per_seq(wits.T, t_new * IDX_HEADS, 1),
        pad_rows(per_seq(kibs, t_new, IDX_DIM), PAGE_SIZE),
        jnp.swapaxes(cache_kidx[0], 1, 2),
        t_new=t_new)
    width = scores.shape[-1]
    sel = _sample_select(scores.reshape(rows_s, width), topk=min(TOPK_MAX, (past + t_new) // 4))
    head_rows = t_new * N_HEADS
    a_s = _sample_attention(
        page_table,
        per_seq(qs, head_rows, HEAD_DIM),
        pad_rows(per_seq(sel, t_new, width), SUBLANES),
        pad_rows(per_seq(kbs, head_rows, HEAD_DIM), PAGE_SIZE * N_HEADS),
        pad_rows(per_seq(vbs, head_rows, HEAD_DIM), PAGE_SIZE * N_HEADS),
        cache_k.reshape(n_pool, PAGE_SIZE * N_HEADS, HEAD_DIM),
        cache_v.reshape(n_pool, PAGE_SIZE * N_HEADS, HEAD_DIM),
        t_new=t_new).reshape(rows_s, ATTN_WIDTH)

    u_ext = jnp.concatenate([state_pool[0], us.reshape(nb, t_new, POOL_WIDTH)], axis=1)
    time_major = lambda z: z.reshape(nb, t_new, z.shape[-1]).transpose(1, 0, 2).reshape(rows_s, z.shape[-1])
    y_tm = _sample_combine(time_major(a_s), time_major(gas), time_major(gbs), u_ext.transpose(1, 0, 2),
                           time_major(xs2d), w_pool_b, ps, w_out_b, g, b, t_new=t_new)
    y_sample = y_tm.reshape(t_new, nb, D_MODEL).transpose(1, 0, 2)

    hd = (N_HEADS, HEAD_DIM)
    return (y_prompt.reshape(batch, seq, D_MODEL),
            y_sample,
            kf.reshape(1, batch, seq, *hd), vf.reshape(1, batch, seq, *hd),
            jnp.swapaxes(kit, 1, 2)[None],
            u.reshape(batch, seq, POOL_WIDTH)[None, :, -POOL_STATE:],
            kfs.reshape(1, nb, t_new, *hd), vfs.reshape(1, nb, t_new, *hd),
            kifs.reshape(1, nb, t_new, IDX_DIM),
            u_ext[None, :, -POOL_STATE:])
```

```python
import functools

import numpy as np
import jax
import jax.numpy as jnp
from jax import lax
from jax.experimental import pallas as pl
from jax.experimental.pallas import tpu as pltpu
from jax.experimental.pallas import tpu_sc as plsc

F32 = jnp.float32
BF16 = jnp.bfloat16
I32 = jnp.int32

D_MODEL = 2048
PAGE_SIZE = 128
N_HEADS = 8
HEAD_DIM = 128
ATTN_WIDTH = N_HEADS * HEAD_DIM
ROT_DIM = HEAD_DIM // 4
ROPE_THETA = 500000.0
IDX_HEADS = 16
IDX_DIM = 64
IDX_ROT_DIM = IDX_DIM // 4
TOPK_MAX = 256
POOL_WINDOWS = (2, 4, 8, 16)
N_POOL_GROUPS = len(POOL_WINDOWS)
POOL_WIDTH = D_MODEL - ATTN_WIDTH
POOL_GROUP_DIM = POOL_WIDTH // N_POOL_GROUPS
POOL_STATE = max(POOL_WINDOWS) - 1
IN_SPLITS = (ATTN_WIDTH, ATTN_WIDTH, ATTN_WIDTH, ATTN_WIDTH,
             IDX_HEADS * IDX_DIM, IDX_DIM, IDX_HEADS, POOL_WIDTH, POOL_WIDTH)
IN_COLS = sum(IN_SPLITS)
DEPTH = 1
DEEPNORM_ALPHA = (2 * DEPTH) ** 0.25
LN_EPS = 1e-5
ATTN_SCALE = HEAD_DIM ** -0.5
IDX_SCALE = (IDX_DIM ** -0.5) * (IDX_HEADS ** -0.5)

LANES = 128
SUBLANES = 8
VMEM_LIMIT = 56 * 1024 * 1024

(_OFF_Q, _OFF_K, _OFF_V, _OFF_GA, _OFF_QI, _OFF_KI, _OFF_WI, _OFF_U, _OFF_GB) = (
    [0] + np.cumsum(IN_SPLITS)[:-1].tolist())

NEG_INF = float("-inf")
M_INIT = -1e30
INT_MIN = -2 ** 31
KEY_OF_NEG_INF = -2139095041

KEY_CHUNK = 256
HALF_CHUNK = 128
Q_TILE = 256
PAGES_PER_STEP = 16


def _dot(a, b):
    return jnp.dot(a, b, preferred_element_type=F32)


def _dot_nt(a, b):
    return lax.dot_general(a, b, (((1,), (1,)), ((), ())), preferred_element_type=F32)


def _sigmoid(x):
    return 1.0 / (1.0 + jnp.exp(-x))


def _rope_tables_t(pos, rot):
    half = rot // 2
    freqs = ROPE_THETA ** (-jnp.arange(half, dtype=F32) * (2.0 / rot))
    ang = pos.astype(F32)[:, None] * freqs[None, :]
    return jnp.stack([jnp.cos(ang).T, jnp.sin(ang).T]).astype(F32)


def _rope_tables(pos, rot, group):
    half = rot // 2
    freqs = ROPE_THETA ** (-jnp.arange(half, dtype=F32) * (2.0 / rot))
    ang = pos.astype(F32)[:, None] * freqs[None, :]
    cos, sin = jnp.cos(ang), jnp.sin(ang)
    lane = np.arange(LANES) % group
    sel = lane % half
    cos_l, sin_l = cos[:, sel], sin[:, sel]
    first = jnp.asarray(lane < half)[None, :]
    second = jnp.asarray((lane >= half) & (lane < rot))[None, :]
    c = jnp.where(first | second, cos_l, 1.0)
    s1 = jnp.where(first, -sin_l, 0.0)
    s2 = jnp.where(second, sin_l, 0.0)
    return jnp.stack([c, s1, s2]).astype(F32)


def _rope(x, tab_ref, half):
    fwd = pltpu.roll(x, LANES - half, 1)
    bwd = pltpu.roll(x, half, 1)
    return x * tab_ref[0] + fwd * tab_ref[1] + bwd * tab_ref[2]


def _proj_kernel(head_major, x_ref, wt_ref, th_ref, ti_ref, tit_ref,
                 q_ref, kf_ref, kb_ref, vf_ref, vb_ref, ga_ref, qi_ref, kif_ref, kib_ref, kit_ref,
                 wit_ref, u_ref, gb_ref):
    xb = x_ref[...].astype(BF16)

    def seg(off, n):
        return _dot_nt(xb, wt_ref[off:off + n, :])

    def seg_t(off, n):
        return _dot_nt(wt_ref[off:off + n, :], xb)

    qf, kf = seg(_OFF_Q, ATTN_WIDTH), seg(_OFF_K, ATTN_WIDTH)
    for h in range(N_HEADS):
        sl = slice(h * HEAD_DIM, (h + 1) * HEAD_DIM)
        qh = _rope(qf[:, sl], th_ref, ROT_DIM // 2)
        q_ref[:, sl] = (qh * ATTN_SCALE).astype(BF16)
        kh = _rope(kf[:, sl], th_ref, ROT_DIM // 2)
        kf_ref[:, sl] = kh
        kb_ref[:, sl] = kh.astype(BF16)

    v = seg(_OFF_V, ATTN_WIDTH)
    vf_ref[...] = v
    if head_major:
        vb_ref[0] = v.T.astype(BF16)
    else:
        vb_ref[...] = v.astype(BF16)
    ga_ref[...] = seg(_OFF_GA, ATTN_WIDTH)
    u_ref[...] = seg(_OFF_U, POOL_WIDTH)
    gb_ref[...] = seg(_OFF_GB, POOL_WIDTH)

    qif = seg(_OFF_QI, IDX_HEADS * IDX_DIM)
    for j in range(IDX_HEADS // 2):
        r = _rope(qif[:, j * LANES:(j + 1) * LANES], ti_ref, IDX_ROT_DIM // 2).astype(BF16)
        if head_major:
            qi_ref[2 * j] = r[:, :IDX_DIM]
            qi_ref[2 * j + 1] = r[:, IDX_DIM:]
        else:
            qi_ref[:, j * LANES:(j + 1) * LANES] = r

    half = IDX_ROT_DIM // 2
    kt = seg_t(_OFF_KI, IDX_DIM)
    x1, x2 = kt[:half], kt[half:2 * half]
    cos_t, sin_t = tit_ref[0], tit_ref[1]
    kit_ref[0] = jnp.concatenate([x1 * cos_t - x2 * sin_t, x1 * sin_t + x2 * cos_t, kt[2 * half:]], axis=0)
    kn = jnp.concatenate([seg(_OFF_KI, IDX_DIM), jnp.zeros((xb.shape[0], LANES - IDX_DIM), F32)], axis=1)
    ki = _rope(kn, ti_ref, half)[:, :IDX_DIM]
    kif_ref[...] = ki
    kib_ref[...] = ki.astype(BF16)

    wit_ref[...] = seg_t(_OFF_WI, IDX_HEADS) * IDX_SCALE


def _project(x2d, w_t, pos, *, tm, rows_per_seq, head_major):
    rows = x2d.shape[0]
    nb = rows // tm
    pos_blocks = pos.shape[0] // tm
    seq_blocks = rows_per_seq // tm
    tab_h = _rope_tables(pos, ROT_DIM, HEAD_DIM)
    tab_i = _rope_tables(pos, IDX_ROT_DIM, IDX_DIM)
    tab_it = _rope_tables_t(pos, IDX_ROT_DIM)

    row_blk = lambda n: pl.BlockSpec((tm, n), lambda i: (i, 0))
    tab_blk = pl.BlockSpec((3, tm, LANES), lambda i: (0, i % pos_blocks, 0))
    tabt_blk = pl.BlockSpec((2, IDX_ROT_DIM // 2, tm), lambda i: (0, 0, i % pos_blocks))
    wide = lambda dt: jax.ShapeDtypeStruct((rows, ATTN_WIDTH), dt)
    if head_major:
        qi_shape = jax.ShapeDtypeStruct((IDX_HEADS, rows, IDX_DIM), BF16)
        qi_blk = pl.BlockSpec((IDX_HEADS, tm, IDX_DIM), lambda i: (0, i, 0))
        vb_shape = jax.ShapeDtypeStruct((nb, ATTN_WIDTH, tm), BF16)
        vb_blk = pl.BlockSpec((1, ATTN_WIDTH, tm), lambda i: (i, 0, 0))
    else:
        qi_shape = jax.ShapeDtypeStruct((rows, IDX_HEADS * IDX_DIM), BF16)
        qi_blk = row_blk(IDX_HEADS * IDX_DIM)
        vb_shape, vb_blk = wide(BF16), row_blk(ATTN_WIDTH)

    out_shape = (wide(BF16), wide(F32), wide(BF16), wide(F32), vb_shape, wide(F32),
                 qi_shape,
                 jax.ShapeDtypeStruct((rows, IDX_DIM), F32), jax.ShapeDtypeStruct((rows, IDX_DIM), BF16),
                 jax.ShapeDtypeStruct((rows // rows_per_seq, IDX_DIM, rows_per_seq), F32),
                 jax.ShapeDtypeStruct((IDX_HEADS, rows), F32), wide(F32), wide(F32))
    out_specs = (row_blk(ATTN_WIDTH),) * 4 + (vb_blk, row_blk(ATTN_WIDTH)) + (
        qi_blk, row_blk(IDX_DIM), row_blk(IDX_DIM),
        pl.BlockSpec((1, IDX_DIM, tm), lambda i: (i // seq_blocks, 0, i % seq_blocks)),
        pl.BlockSpec((IDX_HEADS, tm), lambda i: (0, i)),
        row_blk(POOL_WIDTH), row_blk(POOL_WIDTH))
    return pl.pallas_call(
        functools.partial(_proj_kernel, head_major),
        grid=(nb,),
        in_specs=[row_blk(D_MODEL),
                  pl.BlockSpec((IN_COLS, D_MODEL), lambda i: (0, 0), pipeline_mode=pl.Buffered(1)),
                  tab_blk, tab_blk, tabt_blk],
        out_specs=out_specs,
        out_shape=out_shape,
        compiler_params=pltpu.CompilerParams(dimension_semantics=("arbitrary",),
                                             vmem_limit_bytes=VMEM_LIMIT),
        name="proj_hm" if head_major else "proj_rm",
    )(x2d, w_t, tab_h, tab_i, tab_it)


def _key_to_float(key):
    return pltpu.bitcast(key ^ ((key >> 31) & 0x7FFFFFFF), F32)


def _kth_largest(count_ge, topk, shape):
    def bit_body(it, prefix):
        trial = prefix | jnp.left_shift(jnp.int32(1), 31 - it)
        cnt = count_ge(_key_to_float(trial ^ INT_MIN))
        return jnp.where(cnt >= topk, trial, prefix)

    prefix = lax.fori_loop(0, 32, bit_body, jnp.zeros(shape, I32))
    return _key_to_float(jnp.maximum(prefix ^ INT_MIN, KEY_OF_NEG_INF))


def _fold_rows(x, op):
    return op(x.reshape(x.shape[0] // SUBLANES, SUBLANES, x.shape[1]), axis=0)


def _prompt_attn_kernel(topk, q_ref, qi_ref, wit_ref, ki_ref, k_ref, vt_ref, a_ref,
                        sc_s, bias_s, acc_s):
    i = pl.program_id(1)
    n_chunks = (i * Q_TILE + Q_TILE + KEY_CHUNK - 1) // KEY_CHUNK
    qpos = i * Q_TILE + lax.broadcasted_iota(I32, (1, Q_TILE), 1)
    row = lax.broadcasted_iota(I32, (HALF_CHUNK, 1), 0)
    halves = [slice(r * HALF_CHUNK, (r + 1) * HALF_CHUNK) for r in range(KEY_CHUNK // HALF_CHUNK)]

    def score_chunk(c, carry):
        for rows in halves:
            kic = ki_ref[c, rows, :]
            acc = jnp.zeros((HALF_CHUNK, Q_TILE), F32)
            for h in range(IDX_HEADS):
                st = _dot_nt(kic, qi_ref[h])
                acc = acc + jnp.maximum(st, 0.0) * wit_ref[h:h + 1, :]
            causal = (c * KEY_CHUNK + rows.start + row) <= qpos
            sc_s[c, rows, :] = jnp.where(causal, acc, NEG_INF)
        return carry

    lax.fori_loop(0, n_chunks, score_chunk, 0)

    def count(pred):
        def body(c, cnt):
            for rows in halves:
                cnt = cnt + _fold_rows(pred(sc_s[c, rows, :]).astype(F32), jnp.sum)
            return cnt
        cnt = lax.fori_loop(0, n_chunks, body, jnp.zeros((SUBLANES, Q_TILE), F32))
        return jnp.sum(cnt, axis=0, keepdims=True)

    thr = _kth_largest(lambda cand: count(lambda x: x >= cand), topk, (1, Q_TILE))

    surplus = (count(lambda x: x >= thr) > topk) & (thr > NEG_INF)
    has_ties = jnp.max(surplus.astype(F32)) > 0.5

    @pl.when(jnp.logical_not(has_ties))
    def _():
        def bias_chunk(c, carry):
            for rows in halves:
                sel = (sc_s[c, rows, :] >= thr) & ((c * KEY_CHUNK + rows.start + row) <= qpos)
                bias_s[c, rows, :] = jnp.where(sel, 0.0, NEG_INF).astype(F32)
            return carry

        lax.fori_loop(0, n_chunks, bias_chunk, 0)

    @pl.when(has_ties)
    def _():
        quota = topk - count(lambda x: x > thr)
        earlier = (lax.broadcasted_iota(I32, (HALF_CHUNK, HALF_CHUNK), 0)
                   > lax.broadcasted_iota(I32, (HALF_CHUNK, HALF_CHUNK), 1)).astype(BF16)

        def bias_chunk(c, seen):
            for rows in halves:
                x = sc_s[c, rows, :]
                tie = (x == thr).astype(BF16)
                before = seen + _dot(earlier, tie)
                sel = ((x > thr) | ((x == thr) & (before < quota))) \
                    & ((c * KEY_CHUNK + rows.start + row) <= qpos)
                bias_s[c, rows, :] = jnp.where(sel, 0.0, NEG_INF).astype(F32)
                seen = seen + jnp.sum(tie.astype(F32), axis=0, keepdims=True)
            return seen

        lax.fori_loop(0, n_chunks, bias_chunk, jnp.zeros((1, Q_TILE), F32))

    acc_s[...] = jnp.zeros(acc_s.shape, F32)
    lane_groups = [slice(g * LANES, (g + 1) * LANES) for g in range(Q_TILE // LANES)]
    n_state = N_HEADS * len(lane_groups)

    def attn_chunk(c, state):
        ms, ls = state
        new_ms, new_ls = [], []
        for h in range(N_HEADS):
            sl = slice(h * HEAD_DIM, (h + 1) * HEAD_DIM)
            for g, qs in enumerate(lane_groups):
                m_old, l_old = ms[h * len(lane_groups) + g], ls[h * len(lane_groups) + g]
                s = _dot_nt(k_ref[c, :, sl], q_ref[qs, sl]) + bias_s[c, :, qs]
                m_new = jnp.maximum(m_old, jnp.max(_fold_rows(s, jnp.max), axis=0, keepdims=True))
                alpha = jnp.exp(m_old - m_new)
                p = jnp.exp(s - m_new)
                new_ms.append(m_new)
                new_ls.append(alpha * l_old + _fold_rows(p, jnp.sum))
                acc_s[h, :, qs] = alpha * acc_s[h, :, qs] + _dot(vt_ref[c, sl, :], p.astype(BF16))
        return tuple(new_ms), tuple(new_ls)

    _, ls = lax.fori_loop(
        0, n_chunks, attn_chunk,
        (tuple(jnp.full((1, LANES), M_INIT, F32) for _ in range(n_state)),
         tuple(jnp.zeros((SUBLANES, LANES), F32) for _ in range(n_state))))

    for h in range(N_HEADS):
        l = jnp.concatenate([jnp.sum(ls[h * len(lane_groups) + g], axis=0, keepdims=True)
                             for g in range(len(lane_groups))], axis=1)
        a_ref[:, h * HEAD_DIM:(h + 1) * HEAD_DIM] = (acc_s[h] / l).T


def _prompt_attention(q, qi_hm, wit, ki_b, k_b, vt3, *, batch, seq):
    nq = seq // Q_TILE
    nch = seq // KEY_CHUNK
    topk = min(TOPK_MAX, seq // 4)
    ki3 = ki_b.reshape(batch * nch, KEY_CHUNK, IDX_DIM)
    k3 = k_b.reshape(batch * nch, KEY_CHUNK, ATTN_WIDTH)
    resident = lambda r, n: pl.BlockSpec((nch, r, n), lambda b, i: (b, 0, 0),
                                         pipeline_mode=pl.Buffered(1))
    return pl.pallas_call(
        functools.partial(_prompt_attn_kernel, topk),
        grid=(batch, nq),
        in_specs=[pl.BlockSpec((Q_TILE, ATTN_WIDTH), lambda b, i: (b * nq + i, 0)),
                  pl.BlockSpec((IDX_HEADS, Q_TILE, IDX_DIM), lambda b, i: (0, b * nq + i, 0)),
                  pl.BlockSpec((IDX_HEADS, Q_TILE), lambda b, i: (0, b * nq + i)),
                  resident(KEY_CHUNK, IDX_DIM), resident(KEY_CHUNK, ATTN_WIDTH),
                  resident(ATTN_WIDTH, KEY_CHUNK)],
        out_specs=pl.BlockSpec((Q_TILE, ATTN_WIDTH), lambda b, i: (b * nq + i, 0)),
        out_shape=jax.ShapeDtypeStruct((batch * seq, ATTN_WIDTH), F32),
        scratch_shapes=[pltpu.VMEM((nch, KEY_CHUNK, Q_TILE), F32),
                        pltpu.VMEM((nch, KEY_CHUNK, Q_TILE), F32),
                        pltpu.VMEM((N_HEADS, HEAD_DIM, Q_TILE), F32)],
        compiler_params=pltpu.CompilerParams(dimension_semantics=("arbitrary", "arbitrary"),
                                             vmem_limit_bytes=VMEM_LIMIT),
        name="prompt_attn",
    )(q, qi_hm, wit, ki3, k3, vt3)


HALO = 16


def _gate_out_norm(a, ga, p, gb, x, wo_ref, g_ref, b_ref):
    mixed = jnp.concatenate([a * (ga * _sigmoid(ga)), p * (gb * _sigmoid(gb))], axis=-1)
    h = DEEPNORM_ALPHA * x + _dot(mixed.astype(BF16), wo_ref[...])
    mu = jnp.mean(h, axis=-1, keepdims=True)
    d = h - mu
    var = jnp.mean(d * d, axis=-1, keepdims=True)
    return d * lax.rsqrt(var + LN_EPS) * g_ref[...] + b_ref[...]


def _combine_kernel(blocks_per_seq, a_ref, ga_ref, gb_ref, u_ref, uh_ref, x_ref,
                    wp_ref, ps_ref, wo_ref, g_ref, b_ref, y_ref, ext_s):
    i = pl.program_id(0)
    tm = u_ref.shape[0]
    first = (i % blocks_per_seq) == 0
    ext_s[0:HALO, :] = jnp.where(first, 0.0, uh_ref[...])
    ext_s[HALO:HALO + tm, :] = u_ref[...]
    pos = (i % blocks_per_seq) * tm + lax.broadcasted_iota(I32, (tm, 1), 0)

    parts = []
    for g, w in enumerate(POOL_WINDOWS):
        cols = slice(g * POOL_GROUP_DIM, (g + 1) * POOL_GROUP_DIM)
        win = ext_s[HALO:HALO + tm, cols]
        for j in range(1, w):
            win = win + ext_s[HALO - j:HALO - j + tm, cols]
        count = jnp.minimum(pos + 1, w).astype(F32)
        pooled = win / count - u_ref[:, cols]
        parts.append(_dot(pooled.astype(BF16), wp_ref[g]))
    p = jnp.concatenate(parts, axis=-1) * ps_ref[...]
    y_ref[...] = _gate_out_norm(a_ref[...], ga_ref[...], p, gb_ref[...], x_ref[...],
                                wo_ref, g_ref, b_ref)


def _combine(a, ga, gb, u, x2d, w_pool_b, pool_scale, w_out_b, ln_g, ln_b, *, seq, tm):
    rows = a.shape[0]
    nb = rows // tm
    bps = seq // tm
    hpb = tm // HALO
    row_blk = lambda n: pl.BlockSpec((tm, n), lambda i: (i, 0))
    const = lambda shape: pl.BlockSpec(shape, lambda i: (0,) * len(shape))
    return pl.pallas_call(
        functools.partial(_combine_kernel, bps),
        grid=(nb,),
        in_specs=[row_blk(ATTN_WIDTH), row_blk(ATTN_WIDTH), row_blk(POOL_WIDTH), row_blk(POOL_WIDTH),
                  pl.BlockSpec((HALO, POOL_WIDTH), lambda i: (jnp.maximum(i * hpb - 1, 0), 0)),
                  row_blk(D_MODEL),
                  const((N_POOL_GROUPS, POOL_GROUP_DIM, POOL_GROUP_DIM)), const((1, POOL_WIDTH)),
                  const((D_MODEL, D_MODEL)), const((1, D_MODEL)), const((1, D_MODEL))],
        out_specs=row_blk(D_MODEL),
        out_shape=jax.ShapeDtypeStruct((rows, D_MODEL), F32),
        scratch_shapes=[pltpu.VMEM((HALO + tm, POOL_WIDTH), F32)],
        compiler_params=pltpu.CompilerParams(dimension_semantics=("arbitrary",),
                                             vmem_limit_bytes=VMEM_LIMIT),
        name="combine",
    )(a, ga, gb, u, u, x2d, w_pool_b, pool_scale, w_out_b, ln_g, ln_b)


def _sample_scores_kernel(n_pages, t_new, pt_ref, qi_ref, w_ref, kn_ref, *refs):
    page_refs, sc_ref = refs[:n_pages], refs[n_pages]
    qi = qi_ref[0]
    w = w_ref[0]

    def head_sum(s):
        r = jnp.maximum(s, 0.0) * w
        return jnp.sum(r.reshape(t_new, IDX_HEADS, s.shape[-1]), axis=1)

    for p in range(n_pages):
        kp_t = page_refs[p][0].astype(BF16)
        sc_ref[0, :, p * PAGE_SIZE:(p + 1) * PAGE_SIZE] = head_sum(_dot(qi, kp_t))
    s_new = head_sum(_dot_nt(qi, kn_ref[0]))
    t_idx = lax.broadcasted_iota(I32, s_new.shape, 0)
    j_idx = lax.broadcasted_iota(I32, s_new.shape, 1)
    sc_ref[0, :, n_pages * PAGE_SIZE:] = jnp.where(j_idx <= t_idx, s_new, NEG_INF)


def _sample_scores(page_table, qi_rows, w_col, ki_new_pad, cache_kidx_t, *, t_new):
    nb, n_pages = page_table.shape
    width = (n_pages + 1) * PAGE_SIZE
    page_spec = lambda p: pl.BlockSpec((1, IDX_DIM, PAGE_SIZE), lambda b, pt, p=p: (pt[b, p], 0, 0))
    grid_spec = pltpu.PrefetchScalarGridSpec(
        num_scalar_prefetch=1,
        grid=(nb,),
        in_specs=[pl.BlockSpec((1, t_new * IDX_HEADS, IDX_DIM), lambda b, pt: (b, 0, 0)),
                  pl.BlockSpec((1, t_new * IDX_HEADS, 1), lambda b, pt: (b, 0, 0)),
                  pl.BlockSpec((1, PAGE_SIZE, IDX_DIM), lambda b, pt: (b, 0, 0))]
                 + [page_spec(p) for p in range(n_pages)],
        out_specs=pl.BlockSpec((1, t_new, width), lambda b, pt: (b, 0, 0)),
    )
    return pl.pallas_call(
        functools.partial(_sample_scores_kernel, n_pages, t_new),
        grid_spec=grid_spec,
        out_shape=jax.ShapeDtypeStruct((nb, t_new, width), F32),
        compiler_params=pltpu.CompilerParams(dimension_semantics=("arbitrary",),
                                             vmem_limit_bytes=VMEM_LIMIT),
        name="sample_scores",
    )(page_table, qi_rows, w_col, ki_new_pad, *([cache_kidx_t] * n_pages))


def _sample_select_kernel(topk, sc_ref, sel_ref):
    rows, width = sc_ref.shape

    def count(pred):
        return jnp.sum(pred(sc_ref[...]).astype(F32), axis=1, keepdims=True)

    thr = _kth_largest(lambda cand: count(lambda x: x >= cand), topk, (rows, 1))

    surplus = (count(lambda x: x >= thr) > topk) & (thr > NEG_INF)
    has_ties = jnp.max(surplus.astype(F32)) > 0.5

    @pl.when(jnp.logical_not(has_ties))
    def _():
        sc = sc_ref[...]
        sel_ref[...] = jnp.where((sc >= thr) & (sc > NEG_INF), 1.0, 0.0).astype(F32)

    @pl.when(has_ties)
    def _():
        quota = topk - count(lambda x: x > thr)
        earlier = (lax.broadcasted_iota(I32, (LANES, LANES), 0)
                   < lax.broadcasted_iota(I32, (LANES, LANES), 1)).astype(BF16)
        seen = jnp.zeros((rows, 1), F32)
        for j in range(width // LANES):
            cols = slice(j * LANES, (j + 1) * LANES)
            x = sc_ref[:, cols]
            tie = (x == thr).astype(BF16)
            before = seen + _dot(tie, earlier)
            sel = ((x > thr) | ((x == thr) & (before < quota))) & (x > NEG_INF)
            sel_ref[:, cols] = jnp.where(sel, 1.0, 0.0).astype(F32)
            seen = seen + jnp.sum(tie.astype(F32), axis=1, keepdims=True)


def _sample_select(scores2d, *, topk):
    return pl.pallas_call(
        functools.partial(_sample_select_kernel, topk),
        out_shape=jax.ShapeDtypeStruct(scores2d.shape, F32),
        compiler_params=pltpu.CompilerParams(vmem_limit_bytes=VMEM_LIMIT),
        name="sample_select",
    )(scores2d)


def _expand_matrix():
    e = np.zeros((2 * PAGE_SIZE, PAGE_SIZE * N_HEADS), np.float32)
    c = np.arange(PAGE_SIZE * N_HEADS)
    e[c // N_HEADS, c] = 1.0
    e[PAGE_SIZE + c % N_HEADS, c] = 1.0
    return jnp.asarray(e, BF16)


def _sample_attn_kernel(n_steps, g_pages, t_new, pt_ref, q_ref, e_ref, *refs):
    k_refs, v_refs = refs[:g_pages], refs[g_pages:2 * g_pages]
    sel_ref, kn_ref, vn_ref, seln_ref, a_ref, m_s, l_s, acc_s = refs[2 * g_pages:]
    step = pl.program_id(1)
    rows = t_new * N_HEADS
    cols = PAGE_SIZE * N_HEADS

    @pl.when(step == 0)
    def _():
        m_s[...] = jnp.full(m_s.shape, M_INIT, F32)
        l_s[...] = jnp.zeros(l_s.shape, F32)
        acc_s[...] = jnp.zeros(acc_s.shape, F32)

    q = q_ref[0]
    head_row = lax.broadcasted_iota(I32, (rows, LANES), 0) % N_HEADS
    head_onehot = (lax.broadcasted_iota(I32, (rows, LANES), 1) == head_row).astype(BF16)

    def attend(ks, vs, sels):
        n = len(ks)
        marks = []
        for sel in sels:
            per_row = jnp.concatenate(
                [jnp.broadcast_to(sel[t:t + 1, :], (N_HEADS, PAGE_SIZE)) for t in range(t_new)], axis=0)
            marks.append(jnp.concatenate([per_row.astype(BF16), head_onehot], axis=1))
        valid = _dot(jnp.concatenate(marks, axis=0), e_ref[...]) > 1.5
        s = jnp.concatenate(
            [jnp.where(valid[g * rows:(g + 1) * rows], _dot_nt(q, ks[g]), NEG_INF) for g in range(n)],
            axis=1)
        m_old = m_s[...]
        m_new = jnp.maximum(m_old, jnp.max(s, axis=1, keepdims=True))
        alpha = jnp.exp(m_old - m_new)
        pr = jnp.exp(s - m_new)
        l_s[...] = alpha * l_s[...] + jnp.sum(pr, axis=1, keepdims=True)
        pv = _dot(pr[:, :cols].astype(BF16), vs[0])
        for g in range(1, n):
            pv = pv + _dot(pr[:, g * cols:(g + 1) * cols].astype(BF16), vs[g])
        acc_s[...] = alpha * acc_s[...] + pv
        m_s[...] = m_new

    attend([r[0].astype(BF16) for r in k_refs], [r[0].astype(BF16) for r in v_refs],
           [sel_ref[0, :, g * PAGE_SIZE:(g + 1) * PAGE_SIZE] for g in range(g_pages)])

    @pl.when(step == n_steps - 1)
    def _():
        attend([kn_ref[0]], [vn_ref[0]], [seln_ref[0]])
        a_ref[0] = acc_s[...] / l_s[...]


def _sample_attention(page_table, q_rows, sel_pad, k_new_pad, v_new_pad, cache_k2, cache_v2, *, t_new):
    nb, n_pages = page_table.shape
    g_pages = PAGES_PER_STEP
    n_steps = n_pages // g_pages
    rows = t_new * N_HEADS
    cols = PAGE_SIZE * N_HEADS
    page = lambda g: pl.BlockSpec((1, cols, HEAD_DIM), lambda b, s, pt, g=g: (pt[b, s * g_pages + g], 0, 0))
    per_b = lambda r, n: pl.BlockSpec((1, r, n), lambda b, s, pt: (b, 0, 0))
    grid_spec = pltpu.PrefetchScalarGridSpec(
        num_scalar_prefetch=1,
        grid=(nb, n_steps),
        in_specs=[per_b(rows, HEAD_DIM),
                  pl.BlockSpec((2 * PAGE_SIZE, cols), lambda b, s, pt: (0, 0))]
                 + [page(g) for g in range(g_pages)] + [page(g) for g in range(g_pages)]
                 + [pl.BlockSpec((1, SUBLANES, g_pages * PAGE_SIZE), lambda b, s, pt: (b, 0, s)),
                    per_b(cols, HEAD_DIM), per_b(cols, HEAD_DIM),
                    pl.BlockSpec((1, SUBLANES, PAGE_SIZE), lambda b, s, pt: (b, 0, n_pages))],
        out_specs=per_b(rows, HEAD_DIM),
        scratch_shapes=[pltpu.VMEM((rows, 1), F32), pltpu.VMEM((rows, 1), F32),
                        pltpu.VMEM((rows, HEAD_DIM), F32)],
    )
    return pl.pallas_call(
        functools.partial(_sample_attn_kernel, n_steps, g_pages, t_new),
        grid_spec=grid_spec,
        out_shape=jax.ShapeDtypeStruct((nb, rows, HEAD_DIM), F32),
        compiler_params=pltpu.CompilerParams(dimension_semantics=("arbitrary", "arbitrary"),
                                             vmem_limit_bytes=VMEM_LIMIT),
        name="sample_attn",
    )(page_table, q_rows, _expand_matrix(), *([cache_k2] * g_pages), *([cache_v2] * g_pages),
      sel_pad, k_new_pad, v_new_pad, sel_pad)


SC_CORES = 2
SC_SUBCORES = 16
SC_LANES = 16
GATHER_ROWS = 128


def _sample_gather(sel2d, page_table, cache_k_rows, cache_v_rows, *, n_sel):
    nq = sel2d.shape[0]
    nb, n_pages = page_table.shape
    past = n_pages * PAGE_SIZE
    t_new = nq // nb
    rows_per_q = n_sel * N_HEADS
    out = jax.ShapeDtypeStruct((nq * rows_per_q, HEAD_DIM), F32)
    assert nb <= SC_CORES * SC_SUBCORES
    mesh = plsc.VectorSubcoreMesh(core_axis_name="c", subcore_axis_name="s",
                                  num_cores=SC_CORES, num_subcores=SC_SUBCORES)

    def body(sel_hbm, pt_hbm, ck_hbm, cv_hbm, kg_hbm, vg_hbm, sel_v, pt_v, key_v, row_v, buf):
        seq_id = lax.axis_index("c") * SC_SUBCORES + lax.axis_index("s")
        lane = lax.iota(I32, SC_LANES)

        @pl.when(seq_id < nb)
        def _():
            pltpu.sync_copy(pt_hbm.at[seq_id], pt_v)
            gather_sequence(seq_id, lane, sel_hbm, pt_v, ck_hbm, cv_hbm, kg_hbm, vg_hbm,
                            sel_v, key_v, row_v, buf)

    def gather_sequence(seq_id, lane, sel_hbm, pt_v, ck_hbm, cv_hbm, kg_hbm, vg_hbm,
                        sel_v, key_v, row_v, buf):
        for t in range(t_new):
            q = seq_id * t_new + t
            pltpu.sync_copy(sel_hbm.at[q, pl.ds(0, past)], sel_v)
            for j in range(key_v.shape[0] // SC_LANES):
                key_v[pl.ds(j * SC_LANES, SC_LANES)] = jnp.zeros((SC_LANES,), I32)

            def compact(j, cnt):
                m = sel_v[pl.ds(j * SC_LANES, SC_LANES)] > 0.5
                key = j * SC_LANES + lane
                page = plsc.load_gather(pt_v, [lax.shift_right_logical(key, 7)])
                phys = page * PAGE_SIZE + (key & (PAGE_SIZE - 1))
                plsc.store_compressed(key_v.at[pl.ds(cnt, SC_LANES)], phys, mask=m)
                return cnt + jnp.sum(m.astype(I32))

            lax.fori_loop(0, past // SC_LANES, compact, jnp.int32(0))

            def expand(i, carry):
                r = i * SC_LANES + lane
                k = plsc.load_gather(key_v, [lax.shift_right_logical(r, 3)])
                row_v[pl.ds(i * SC_LANES, SC_LANES)] = k * N_HEADS + (r & (N_HEADS - 1))
                return carry

            lax.fori_loop(0, rows_per_q // SC_LANES, expand, 0)

            for w in range(rows_per_q // GATHER_ROWS):
                idx = row_v.at[pl.ds(w * GATHER_ROWS, GATHER_ROWS)]
                dst = pl.ds(q * rows_per_q + w * GATHER_ROWS, GATHER_ROWS)
                pltpu.sync_copy(ck_hbm.at[idx], buf)
                pltpu.sync_copy(buf, kg_hbm.at[dst])
                pltpu.sync_copy(cv_hbm.at[idx], buf)
                pltpu.sync_copy(buf, vg_hbm.at[dst])

    return pl.kernel(
        body, out_type=(out, out), mesh=mesh,
        scratch_types=[pltpu.VMEM((past,), F32), pltpu.VMEM((n_pages,), I32),
                       pltpu.VMEM((n_sel + SC_LANES,), I32), pltpu.VMEM((rows_per_q,), I32),
                       pltpu.VMEM((GATHER_ROWS, HEAD_DIM), F32)],
        compiler_params=pltpu.CompilerParams(needs_layout_passes=False),
        name="sample_gather",
    )(sel2d, page_table, cache_k_rows, cache_v_rows)


def _gathered_attn_kernel(t_new, n_sel, npast_ref, selnew_ref, q_ref, kg_ref, vg_ref, kn_ref, vn_ref, a_ref):
    qi = pl.program_id(0)
    q = q_ref[0]
    cols = n_sel * N_HEADS
    row = lax.broadcasted_iota(I32, (N_HEADS, cols), 0)
    col = lax.broadcasted_iota(I32, (N_HEADS, cols), 1)
    valid = ((col & (N_HEADS - 1)) == row) & (lax.shift_right_logical(col, 3) < npast_ref[qi])
    s = jnp.where(valid, _dot_nt(q, kg_ref[0].astype(BF16)), NEG_INF)

    ncol = t_new * N_HEADS
    rown = lax.broadcasted_iota(I32, (N_HEADS, ncol), 0)
    coln = lax.broadcasted_iota(I32, (N_HEADS, ncol), 1)
    taken = jnp.zeros((N_HEADS, ncol), jnp.bool_)
    for j in range(t_new):
        taken = taken | ((lax.shift_right_logical(coln, 3) == j) & (selnew_ref[qi * t_new + j] > 0))
    sn = jnp.where(taken & ((coln & (N_HEADS - 1)) == rown), _dot_nt(q, kn_ref[0]), NEG_INF)

    m = jnp.maximum(jnp.max(s, axis=1, keepdims=True), jnp.max(sn, axis=1, keepdims=True))
    p, pn = jnp.exp(s - m), jnp.exp(sn - m)
    l = jnp.sum(p, axis=1, keepdims=True) + jnp.sum(pn, axis=1, keepdims=True)
    o = _dot(p.astype(BF16), vg_ref[0].astype(BF16)) + _dot(pn.astype(BF16), vn_ref[0])
    a_ref[0] = o / l


def _gathered_attention(npast, selnew, q_heads, kg, vg, k_new, v_new, *, t_new, n_sel):
    nq = q_heads.shape[0]
    cols = n_sel * N_HEADS
    grid_spec = pltpu.PrefetchScalarGridSpec(
        num_scalar_prefetch=2,
        grid=(nq,),
        in_specs=[pl.BlockSpec((1, N_HEADS, HEAD_DIM), lambda i, a, b: (i, 0, 0)),
                  pl.BlockSpec((1, cols, HEAD_DIM), lambda i, a, b: (i, 0, 0)),
                  pl.BlockSpec((1, cols, HEAD_DIM), lambda i, a, b: (i, 0, 0)),
                  pl.BlockSpec((1, t_new * N_HEADS, HEAD_DIM), lambda i, a, b: (i // t_new, 0, 0)),
                  pl.BlockSpec((1, t_new * N_HEADS, HEAD_DIM), lambda i, a, b: (i // t_new, 0, 0))],
        out_specs=pl.BlockSpec((1, N_HEADS, HEAD_DIM), lambda i, a, b: (i, 0, 0)),
    )
    return pl.pallas_call(
        functools.partial(_gathered_attn_kernel, t_new, n_sel),
        grid_spec=grid_spec,
        out_shape=jax.ShapeDtypeStruct((nq, N_HEADS, HEAD_DIM), F32),
        compiler_params=pltpu.CompilerParams(dimension_semantics=("arbitrary",),
                                             vmem_limit_bytes=VMEM_LIMIT),
        name="gathered_attn",
    )(npast, selnew, q_heads, kg.reshape(nq, cols, HEAD_DIM), vg.reshape(nq, cols, HEAD_DIM), k_new, v_new)


def _sample_combine_kernel(t_new, a_ref, ga_ref, gb_ref, ext_ref, x_ref,
                           wp_ref, ps_ref, wo_ref, g_ref, b_ref, y_ref):
    p_t = []
    for t in range(t_new):
        parts = []
        for g, w in enumerate(POOL_WINDOWS):
            cols = slice(g * POOL_GROUP_DIM, (g + 1) * POOL_GROUP_DIM)
            cur = POOL_STATE + t
            win = ext_ref[cur, :, cols]
            for j in range(1, w):
                win = win + ext_ref[cur - j, :, cols]
            pooled = win / float(w) - ext_ref[cur, :, cols]
            parts.append(_dot(pooled.astype(BF16), wp_ref[g]))
        p_t.append(jnp.concatenate(parts, axis=-1) * ps_ref[...])
    y_ref[...] = _gate_out_norm(a_ref[...], ga_ref[...], jnp.concatenate(p_t, axis=0), gb_ref[...],
                                x_ref[...], wo_ref, g_ref, b_ref)


def _sample_combine(a, ga, gb, ext_tm, x2d, w_pool_b, pool_scale, w_out_b, ln_g, ln_b, *, t_new):
    return pl.pallas_call(
        functools.partial(_sample_combine_kernel, t_new),
        out_shape=jax.ShapeDtypeStruct(x2d.shape, F32),
        compiler_params=pltpu.CompilerParams(vmem_limit_bytes=VMEM_LIMIT),
        name="sample_combine",
    )(a, ga, gb, ext_tm, x2d, w_pool_b, pool_scale, w_out_b, ln_g, ln_b)


def kernel(x_prompt, x_sample, cache_k, cache_v, cache_kidx, state_pool, page_table,
           w_in, w_pool, pool_scale, w_out, ln_g, ln_b):
    assert w_in.shape[0] == DEPTH == 1
    batch, seq, _ = x_prompt.shape
    nb, t_new, _ = x_sample.shape
    n_pool, n_pages = cache_k.shape[1], page_table.shape[1]
    past = n_pages * PAGE_SIZE

    w_t = jnp.swapaxes(w_in[0], 0, 1).astype(BF16)
    w_pool_b = w_pool[0].astype(BF16)
    w_out_b = w_out[0].astype(BF16)
    ps, g, b = pool_scale[0][None, :], ln_g[0][None, :], ln_b[0][None, :]

    xp2d = x_prompt.reshape(batch * seq, D_MODEL)
    (q, kf, kb, vf, vb, ga, qi_hm, _, kib, kit, wit, u, gb) = _project(
        xp2d, w_t, jnp.arange(seq), tm=KEY_CHUNK, rows_per_seq=seq, head_major=True)
    a = _prompt_attention(q, qi_hm, wit, kib, kb, vb, batch=batch, seq=seq)
    y_prompt = _combine(a, ga, gb, u, xp2d, w_pool_b, ps, w_out_b, g, b, seq=seq, tm=256)

    rows_s = nb * t_new
    xs2d = x_sample.reshape(rows_s, D_MODEL)
    (qs, kfs, kbs, vfs, vbs, gas, qis, kifs, kibs, _, wits, us, gbs) = _project(
        xs2d, w_t, jnp.tile(past + jnp.arange(t_new), nb), tm=rows_s, rows_per_seq=rows_s,
        head_major=False)

    per_seq = lambda z, r, n: z.reshape(nb, r, n)
    pad_rows = lambda z, n: jnp.pad(z, ((0, 0), (0, n - z.shape[1]), (0, 0)))
    scores = _sample_scores(
        page_table,
        per_seq(qis, t_new * IDX_HEADS, IDX_DIM),
        per_seq(wits.T, t_new * IDX_HEADS, 1),
        pad_rows(per_seq(kibs, t_new, IDX_DIM), PAGE_SIZE),
        jnp.swapaxes(cache_kidx[0], 1, 2),
        t_new=t_new)
    width = scores.shape[-1]
    sel = _sample_select(scores.reshape(rows_s, width), topk=min(TOPK_MAX, (past + t_new) // 4))
    head_rows = t_new * N_HEADS
    n_sel = min(TOPK_MAX, (past + t_new) // 4)
    kg, vg = _sample_gather(sel, page_table,
                            cache_k.reshape(n_pool * PAGE_SIZE * N_HEADS, HEAD_DIM),
                            cache_v.reshape(n_pool * PAGE_SIZE * N_HEADS, HEAD_DIM), n_sel=n_sel)
    a_s = _gathered_attention(
        jnp.sum(sel[:, :past], axis=1).astype(I32),
        sel[:, past:past + t_new].astype(I32).reshape(-1),
        qs.reshape(rows_s, N_HEADS, HEAD_DIM), kg, vg,
        per_seq(kbs, head_rows, HEAD_DIM), per_seq(vbs, head_rows, HEAD_DIM),
        t_new=t_new, n_sel=n_sel).reshape(rows_s, ATTN_WIDTH)

    u_ext = jnp.concatenate([state_pool[0], us.reshape(nb, t_new, POOL_WIDTH)], axis=1)
    time_major = lambda z: z.reshape(nb, t_new, z.shape[-1]).transpose(1, 0, 2).reshape(rows_s, z.shape[-1])
    y_tm = _sample_combine(time_major(a_s), time_major(gas), time_major(gbs), u_ext.transpose(1, 0, 2),
                           time_major(xs2d), w_pool_b, ps, w_out_b, g, b, t_new=t_new)
    y_sample = y_tm.reshape(t_new, nb, D_MODEL).transpose(1, 0, 2)

    hd = (N_HEADS, HEAD_DIM)
    return (y_prompt.reshape(batch, seq, D_MODEL),
            y_sample,
            kf.reshape(1, batch, seq, *hd), vf.reshape(1, batch, seq, *hd),
            jnp.swapaxes(kit, 1, 2)[None],
            u.reshape(batch, seq, POOL_WIDTH)[None, :, -POOL_STATE:],
            kfs.reshape(1, nb, t_new, *hd), vfs.reshape(1, nb, t_new, *hd),
            kifs.reshape(1, nb, t_new, IDX_DIM),
            u_ext[None, :, -POOL_STATE:])
```

```python
import functools

import numpy as np
import jax
import jax.numpy as jnp
from jax import lax
from jax.experimental import pallas as pl
from jax.experimental.pallas import tpu as pltpu
from jax.experimental.pallas import tpu_sc as plsc

F32 = jnp.float32
BF16 = jnp.bfloat16
I32 = jnp.int32

D_MODEL = 2048
PAGE_SIZE = 128
N_HEADS = 8
HEAD_DIM = 128
ATTN_WIDTH = N_HEADS * HEAD_DIM
ROT_DIM = HEAD_DIM // 4
ROPE_THETA = 500000.0
IDX_HEADS = 16
IDX_DIM = 64
IDX_ROT_DIM = IDX_DIM // 4
TOPK_MAX = 256
POOL_WINDOWS = (2, 4, 8, 16)
N_POOL_GROUPS = len(POOL_WINDOWS)
POOL_WIDTH = D_MODEL - ATTN_WIDTH
POOL_GROUP_DIM = POOL_WIDTH // N_POOL_GROUPS
POOL_STATE = max(POOL_WINDOWS) - 1
IN_SPLITS = (ATTN_WIDTH, ATTN_WIDTH, ATTN_WIDTH, ATTN_WIDTH,
             IDX_HEADS * IDX_DIM, IDX_DIM, IDX_HEADS, POOL_WIDTH, POOL_WIDTH)
IN_COLS = sum(IN_SPLITS)
DEPTH = 1
DEEPNORM_ALPHA = (2 * DEPTH) ** 0.25
LN_EPS = 1e-5
ATTN_SCALE = HEAD_DIM ** -0.5
IDX_SCALE = (IDX_DIM ** -0.5) * (IDX_HEADS ** -0.5)

LANES = 128
SUBLANES = 8
VMEM_LIMIT = 56 * 1024 * 1024

(_OFF_Q, _OFF_K, _OFF_V, _OFF_GA, _OFF_QI, _OFF_KI, _OFF_WI, _OFF_U, _OFF_GB) = (
    [0] + np.cumsum(IN_SPLITS)[:-1].tolist())

NEG_INF = float("-inf")
M_INIT = -1e30
INT_MIN = -2 ** 31
KEY_OF_NEG_INF = -2139095041

KEY_CHUNK = 256
HALF_CHUNK = 128
Q_TILE = 256


def _dot(a, b):
    return jnp.dot(a, b, preferred_element_type=F32)


def _dot_nt(a, b):
    return lax.dot_general(a, b, (((1,), (1,)), ((), ())), preferred_element_type=F32)


def _sigmoid(x):
    return 1.0 / (1.0 + jnp.exp(-x))


def _rope_tables_t(pos, rot):
    half = rot // 2
    freqs = ROPE_THETA ** (-jnp.arange(half, dtype=F32) * (2.0 / rot))
    ang = pos.astype(F32)[:, None] * freqs[None, :]
    return jnp.stack([jnp.cos(ang).T, jnp.sin(ang).T]).astype(F32)


def _rope_tables(pos, rot, group):
    half = rot // 2
    freqs = ROPE_THETA ** (-jnp.arange(half, dtype=F32) * (2.0 / rot))
    ang = pos.astype(F32)[:, None] * freqs[None, :]
    cos, sin = jnp.cos(ang), jnp.sin(ang)
    lane = np.arange(LANES) % group
    sel = lane % half
    cos_l, sin_l = cos[:, sel], sin[:, sel]
    first = jnp.asarray(lane < half)[None, :]
    second = jnp.asarray((lane >= half) & (lane < rot))[None, :]
    c = jnp.where(first | second, cos_l, 1.0)
    s1 = jnp.where(first, -sin_l, 0.0)
    s2 = jnp.where(second, sin_l, 0.0)
    return jnp.stack([c, s1, s2]).astype(F32)


def _rope(x, tab_ref, half):
    fwd = pltpu.roll(x, LANES - half, 1)
    bwd = pltpu.roll(x, half, 1)
    return x * tab_ref[0] + fwd * tab_ref[1] + bwd * tab_ref[2]


def _proj_kernel(head_major, x_ref, wt_ref, th_ref, ti_ref, tit_ref,
                 q_ref, kf_ref, kb_ref, vf_ref, vb_ref, ga_ref, qi_ref, kif_ref, kib_ref, kit_ref,
                 wit_ref, u_ref, gb_ref):
    xb = x_ref[...].astype(BF16)

    def seg(off, n):
        return _dot_nt(xb, wt_ref[off:off + n, :])

    def seg_t(off, n):
        return _dot_nt(wt_ref[off:off + n, :], xb)

    qf, kf = seg(_OFF_Q, ATTN_WIDTH), seg(_OFF_K, ATTN_WIDTH)
    for h in range(N_HEADS):
        sl = slice(h * HEAD_DIM, (h + 1) * HEAD_DIM)
        qh = _rope(qf[:, sl], th_ref, ROT_DIM // 2)
        q_ref[:, sl] = (qh * ATTN_SCALE).astype(BF16)
        kh = _rope(kf[:, sl], th_ref, ROT_DIM // 2)
        kf_ref[:, sl] = kh
        kb_ref[:, sl] = kh.astype(BF16)

    v = seg(_OFF_V, ATTN_WIDTH)
    vf_ref[...] = v
    if head_major:
        vb_ref[0] = v.T.astype(BF16)
    else:
        vb_ref[...] = v.astype(BF16)
    ga_ref[...] = seg(_OFF_GA, ATTN_WIDTH)
    u_ref[...] = seg(_OFF_U, POOL_WIDTH)
    gb_ref[...] = seg(_OFF_GB, POOL_WIDTH)

    qif = seg(_OFF_QI, IDX_HEADS * IDX_DIM)
    for j in range(IDX_HEADS // 2):
        r = _rope(qif[:, j * LANES:(j + 1) * LANES], ti_ref, IDX_ROT_DIM // 2).astype(BF16)
        if head_major:
            qi_ref[2 * j] = r[:, :IDX_DIM]
            qi_ref[2 * j + 1] = r[:, IDX_DIM:]
        else:
            qi_ref[:, j * LANES:(j + 1) * LANES] = r

    half = IDX_ROT_DIM // 2
    kt = seg_t(_OFF_KI, IDX_DIM)
    x1, x2 = kt[:half], kt[half:2 * half]
    cos_t, sin_t = tit_ref[0], tit_ref[1]
    kit_ref[0] = jnp.concatenate([x1 * cos_t - x2 * sin_t, x1 * sin_t + x2 * cos_t, kt[2 * half:]], axis=0)
    kn = jnp.concatenate([seg(_OFF_KI, IDX_DIM), jnp.zeros((xb.shape[0], LANES - IDX_DIM), F32)], axis=1)
    ki = _rope(kn, ti_ref, half)[:, :IDX_DIM]
    kif_ref[...] = ki
    kib_ref[...] = ki.astype(BF16)

    wit_ref[...] = seg_t(_OFF_WI, IDX_HEADS) * IDX_SCALE


def _project(x2d, w_t, pos, *, tm, rows_per_seq, head_major):
    rows = x2d.shape[0]
    nb = rows // tm
    pos_blocks = pos.shape[0] // tm
    seq_blocks = rows_per_seq // tm
    tab_h = _rope_tables(pos, ROT_DIM, HEAD_DIM)
    tab_i = _rope_tables(pos, IDX_ROT_DIM, IDX_DIM)
    tab_it = _rope_tables_t(pos, IDX_ROT_DIM)

    row_blk = lambda n: pl.BlockSpec((tm, n), lambda i: (i, 0))
    tab_blk = pl.BlockSpec((3, tm, LANES), lambda i: (0, i % pos_blocks, 0))
    tabt_blk = pl.BlockSpec((2, IDX_ROT_DIM // 2, tm), lambda i: (0, 0, i % pos_blocks))
    wide = lambda dt: jax.ShapeDtypeStruct((rows, ATTN_WIDTH), dt)
    if head_major:
        qi_shape = jax.ShapeDtypeStruct((IDX_HEADS, rows, IDX_DIM), BF16)
        qi_blk = pl.BlockSpec((IDX_HEADS, tm, IDX_DIM), lambda i: (0, i, 0))
        vb_shape = jax.ShapeDtypeStruct((nb, ATTN_WIDTH, tm), BF16)
        vb_blk = pl.BlockSpec((1, ATTN_WIDTH, tm), lambda i: (i, 0, 0))
    else:
        qi_shape = jax.ShapeDtypeStruct((rows, IDX_HEADS * IDX_DIM), BF16)
        qi_blk = row_blk(IDX_HEADS * IDX_DIM)
        vb_shape, vb_blk = wide(BF16), row_blk(ATTN_WIDTH)

    out_shape = (wide(BF16), wide(F32), wide(BF16), wide(F32), vb_shape, wide(F32),
                 qi_shape,
                 jax.ShapeDtypeStruct((rows, IDX_DIM), F32), jax.ShapeDtypeStruct((rows, IDX_DIM), BF16),
                 jax.ShapeDtypeStruct((rows // rows_per_seq, IDX_DIM, rows_per_seq), F32),
                 jax.ShapeDtypeStruct((IDX_HEADS, rows), F32), wide(F32), wide(F32))
    out_specs = (row_blk(ATTN_WIDTH),) * 4 + (vb_blk, row_blk(ATTN_WIDTH)) + (
        qi_blk, row_blk(IDX_DIM), row_blk(IDX_DIM),
        pl.BlockSpec((1, IDX_DIM, tm), lambda i: (i // seq_blocks, 0, i % seq_blocks)),
        pl.BlockSpec((IDX_HEADS, tm), lambda i: (0, i)),
        row_blk(POOL_WIDTH), row_blk(POOL_WIDTH))
    return pl.pallas_call(
        functools.partial(_proj_kernel, head_major),
        grid=(nb,),
        in_specs=[row_blk(D_MODEL),
                  pl.BlockSpec((IN_COLS, D_MODEL), lambda i: (0, 0), pipeline_mode=pl.Buffered(1)),
                  tab_blk, tab_blk, tabt_blk],
        out_specs=out_specs,
        out_shape=out_shape,
        compiler_params=pltpu.CompilerParams(dimension_semantics=("arbitrary",),
                                             vmem_limit_bytes=VMEM_LIMIT),
        name="proj_hm" if head_major else "proj_rm",
    )(x2d, w_t, tab_h, tab_i, tab_it)


def _key_to_float(key):
    return pltpu.bitcast(key ^ ((key >> 31) & 0x7FFFFFFF), F32)


def _kth_largest(count_ge, topk, shape):
    def bit_body(it, prefix):
        trial = prefix | jnp.left_shift(jnp.int32(1), 31 - it)
        cnt = count_ge(_key_to_float(trial ^ INT_MIN))
        return jnp.where(cnt >= topk, trial, prefix)

    prefix = lax.fori_loop(0, 32, bit_body, jnp.zeros(shape, I32))
    return _key_to_float(jnp.maximum(prefix ^ INT_MIN, KEY_OF_NEG_INF))


def _fold_rows(x, op):
    return op(x.reshape(x.shape[0] // SUBLANES, SUBLANES, x.shape[1]), axis=0)


def _prompt_attn_kernel(topk, q_ref, qi_ref, wit_ref, ki_ref, k_ref, vt_ref, a_ref,
                        sc_s, bias_s, acc_s):
    i = pl.program_id(1)
    n_chunks = (i * Q_TILE + Q_TILE + KEY_CHUNK - 1) // KEY_CHUNK
    qpos = i * Q_TILE + lax.broadcasted_iota(I32, (1, Q_TILE), 1)
    row = lax.broadcasted_iota(I32, (HALF_CHUNK, 1), 0)
    halves = [slice(r * HALF_CHUNK, (r + 1) * HALF_CHUNK) for r in range(KEY_CHUNK // HALF_CHUNK)]

    def score_chunk(c, carry):
        for rows in halves:
            kic = ki_ref[c, rows, :]
            acc = jnp.zeros((HALF_CHUNK, Q_TILE), F32)
            for h in range(IDX_HEADS):
                st = _dot_nt(kic, qi_ref[h])
                acc = acc + jnp.maximum(st, 0.0) * wit_ref[h:h + 1, :]
            causal = (c * KEY_CHUNK + rows.start + row) <= qpos
            sc_s[c, rows, :] = jnp.where(causal, acc, NEG_INF)
        return carry

    lax.fori_loop(0, n_chunks, score_chunk, 0)

    def count(pred):
        def body(c, cnt):
            for rows in halves:
                cnt = cnt + _fold_rows(pred(sc_s[c, rows, :]).astype(F32), jnp.sum)
            return cnt
        cnt = lax.fori_loop(0, n_chunks, body, jnp.zeros((SUBLANES, Q_TILE), F32))
        return jnp.sum(cnt, axis=0, keepdims=True)

    thr = _kth_largest(lambda cand: count(lambda x: x >= cand), topk, (1, Q_TILE))

    surplus = (count(lambda x: x >= thr) > topk) & (thr > NEG_INF)
    has_ties = jnp.max(surplus.astype(F32)) > 0.5

    @pl.when(jnp.logical_not(has_ties))
    def _():
        def bias_chunk(c, carry):
            for rows in halves:
                sel = (sc_s[c, rows, :] >= thr) & ((c * KEY_CHUNK + rows.start + row) <= qpos)
                bias_s[c, rows, :] = jnp.where(sel, 0.0, NEG_INF).astype(F32)
            return carry

        lax.fori_loop(0, n_chunks, bias_chunk, 0)

    @pl.when(has_ties)
    def _():
        quota = topk - count(lambda x: x > thr)
        earlier = (lax.broadcasted_iota(I32, (HALF_CHUNK, HALF_CHUNK), 0)
                   > lax.broadcasted_iota(I32, (HALF_CHUNK, HALF_CHUNK), 1)).astype(BF16)

        def bias_chunk(c, seen):
            for rows in halves:
                x = sc_s[c, rows, :]
                tie = (x == thr).astype(BF16)
                before = seen + _dot(earlier, tie)
                sel = ((x > thr) | ((x == thr) & (before < quota))) \
                    & ((c * KEY_CHUNK + rows.start + row) <= qpos)
                bias_s[c, rows, :] = jnp.where(sel, 0.0, NEG_INF).astype(F32)
                seen = seen + jnp.sum(tie.astype(F32), axis=0, keepdims=True)
            return seen

        lax.fori_loop(0, n_chunks, bias_chunk, jnp.zeros((1, Q_TILE), F32))

    acc_s[...] = jnp.zeros(acc_s.shape, F32)
    lane_groups = [slice(g * LANES, (g + 1) * LANES) for g in range(Q_TILE // LANES)]
    n_state = N_HEADS * len(lane_groups)

    def attn_chunk(c, state):
        ms, ls = state
        new_ms, new_ls = [], []
        for h in range(N_HEADS):
            sl = slice(h * HEAD_DIM, (h + 1) * HEAD_DIM)
            for g, qs in enumerate(lane_groups):
                m_old, l_old = ms[h * len(lane_groups) + g], ls[h * len(lane_groups) + g]
                s = _dot_nt(k_ref[c, :, sl], q_ref[qs, sl]) + bias_s[c, :, qs]
                m_new = jnp.maximum(m_old, jnp.max(_fold_rows(s, jnp.max), axis=0, keepdims=True))
                alpha = jnp.exp(m_old - m_new)
                p = jnp.exp(s - m_new)
                new_ms.append(m_new)
                new_ls.append(alpha * l_old + _fold_rows(p, jnp.sum))
                acc_s[h, :, qs] = alpha * acc_s[h, :, qs] + _dot(vt_ref[c, sl, :], p.astype(BF16))
        return tuple(new_ms), tuple(new_ls)

    _, ls = lax.fori_loop(
        0, n_chunks, attn_chunk,
        (tuple(jnp.full((1, LANES), M_INIT, F32) for _ in range(n_state)),
         tuple(jnp.zeros((SUBLANES, LANES), F32) for _ in range(n_state))))

    for h in range(N_HEADS):
        l = jnp.concatenate([jnp.sum(ls[h * len(lane_groups) + g], axis=0, keepdims=True)
                             for g in range(len(lane_groups))], axis=1)
        a_ref[:, h * HEAD_DIM:(h + 1) * HEAD_DIM] = (acc_s[h] / l).T


def _prompt_attention(q, qi_hm, wit, ki_b, k_b, vt3, *, batch, seq):
    nq = seq // Q_TILE
    nch = seq // KEY_CHUNK
    topk = min(TOPK_MAX, seq // 4)
    ki3 = ki_b.reshape(batch * nch, KEY_CHUNK, IDX_DIM)
    k3 = k_b.reshape(batch * nch, KEY_CHUNK, ATTN_WIDTH)
    resident = lambda r, n: pl.BlockSpec((nch, r, n), lambda b, i: (b, 0, 0),
                                         pipeline_mode=pl.Buffered(1))
    return pl.pallas_call(
        functools.partial(_prompt_attn_kernel, topk),
        grid=(batch, nq),
        in_specs=[pl.BlockSpec((Q_TILE, ATTN_WIDTH), lambda b, i: (b * nq + i, 0)),
                  pl.BlockSpec((IDX_HEADS, Q_TILE, IDX_DIM), lambda b, i: (0, b * nq + i, 0)),
                  pl.BlockSpec((IDX_HEADS, Q_TILE), lambda b, i: (0, b * nq + i)),
                  resident(KEY_CHUNK, IDX_DIM), resident(KEY_CHUNK, ATTN_WIDTH),
                  resident(ATTN_WIDTH, KEY_CHUNK)],
        out_specs=pl.BlockSpec((Q_TILE, ATTN_WIDTH), lambda b, i: (b * nq + i, 0)),
        out_shape=jax.ShapeDtypeStruct((batch * seq, ATTN_WIDTH), F32),
        scratch_shapes=[pltpu.VMEM((nch, KEY_CHUNK, Q_TILE), F32),
                        pltpu.VMEM((nch, KEY_CHUNK, Q_TILE), F32),
                        pltpu.VMEM((N_HEADS, HEAD_DIM, Q_TILE), F32)],
        compiler_params=pltpu.CompilerParams(dimension_semantics=("arbitrary", "arbitrary"),
                                             vmem_limit_bytes=VMEM_LIMIT),
        name="prompt_attn",
    )(q, qi_hm, wit, ki3, k3, vt3)


HALO = 16


def _gate_out_norm(a, ga, p, gb, x, wo_ref, g_ref, b_ref):
    mixed = jnp.concatenate([a * (ga * _sigmoid(ga)), p * (gb * _sigmoid(gb))], axis=-1)
    h = DEEPNORM_ALPHA * x + _dot(mixed.astype(BF16), wo_ref[...])
    mu = jnp.mean(h, axis=-1, keepdims=True)
    d = h - mu
    var = jnp.mean(d * d, axis=-1, keepdims=True)
    return d * lax.rsqrt(var + LN_EPS) * g_ref[...] + b_ref[...]


def _combine_kernel(blocks_per_seq, a_ref, ga_ref, gb_ref, u_ref, uh_ref, x_ref,
                    wp_ref, ps_ref, wo_ref, g_ref, b_ref, y_ref, ext_s):
    i = pl.program_id(0)
    tm = u_ref.shape[0]
    first = (i % blocks_per_seq) == 0
    ext_s[0:HALO, :] = jnp.where(first, 0.0, uh_ref[...])
    ext_s[HALO:HALO + tm, :] = u_ref[...]
    pos = (i % blocks_per_seq) * tm + lax.broadcasted_iota(I32, (tm, 1), 0)

    parts = []
    for g, w in enumerate(POOL_WINDOWS):
        cols = slice(g * POOL_GROUP_DIM, (g + 1) * POOL_GROUP_DIM)
        win = ext_s[HALO:HALO + tm, cols]
        for j in range(1, w):
            win = win + ext_s[HALO - j:HALO - j + tm, cols]
        count = jnp.minimum(pos + 1, w).astype(F32)
        pooled = win / count - u_ref[:, cols]
        parts.append(_dot(pooled.astype(BF16), wp_ref[g]))
    p = jnp.concatenate(parts, axis=-1) * ps_ref[...]
    y_ref[...] = _gate_out_norm(a_ref[...], ga_ref[...], p, gb_ref[...], x_ref[...],
                                wo_ref, g_ref, b_ref)


def _combine(a, ga, gb, u, x2d, w_pool_b, pool_scale, w_out_b, ln_g, ln_b, *, seq, tm):
    rows = a.shape[0]
    nb = rows // tm
    bps = seq // tm
    hpb = tm // HALO
    row_blk = lambda n: pl.BlockSpec((tm, n), lambda i: (i, 0))
    const = lambda shape: pl.BlockSpec(shape, lambda i: (0,) * len(shape))
    return pl.pallas_call(
        functools.partial(_combine_kernel, bps),
        grid=(nb,),
        in_specs=[row_blk(ATTN_WIDTH), row_blk(ATTN_WIDTH), row_blk(POOL_WIDTH), row_blk(POOL_WIDTH),
                  pl.BlockSpec((HALO, POOL_WIDTH), lambda i: (jnp.maximum(i * hpb - 1, 0), 0)),
                  row_blk(D_MODEL),
                  const((N_POOL_GROUPS, POOL_GROUP_DIM, POOL_GROUP_DIM)), const((1, POOL_WIDTH)),
                  const((D_MODEL, D_MODEL)), const((1, D_MODEL)), const((1, D_MODEL))],
        out_specs=row_blk(D_MODEL),
        out_shape=jax.ShapeDtypeStruct((rows, D_MODEL), F32),
        scratch_shapes=[pltpu.VMEM((HALO + tm, POOL_WIDTH), F32)],
        compiler_params=pltpu.CompilerParams(dimension_semantics=("arbitrary",),
                                             vmem_limit_bytes=VMEM_LIMIT),
        name="combine",
    )(a, ga, gb, u, u, x2d, w_pool_b, pool_scale, w_out_b, ln_g, ln_b)


def _sample_scores_kernel(n_pages, t_new, pt_ref, qi_ref, w_ref, kn_ref, *refs):
    page_refs, sc_ref = refs[:n_pages], refs[n_pages]
    qi = qi_ref[0]
    w = w_ref[0]

    def head_sum(s):
        r = jnp.maximum(s, 0.0) * w
        return jnp.sum(r.reshape(t_new, IDX_HEADS, s.shape[-1]), axis=1)

    for p in range(n_pages):
        kp_t = page_refs[p][0].astype(BF16)
        sc_ref[0, :, p * PAGE_SIZE:(p + 1) * PAGE_SIZE] = head_sum(_dot(qi, kp_t))
    s_new = head_sum(_dot_nt(qi, kn_ref[0]))
    t_idx = lax.broadcasted_iota(I32, s_new.shape, 0)
    j_idx = lax.broadcasted_iota(I32, s_new.shape, 1)
    sc_ref[0, :, n_pages * PAGE_SIZE:] = jnp.where(j_idx <= t_idx, s_new, NEG_INF)


def _sample_scores(page_table, qi_rows, w_col, ki_new_pad, cache_kidx_t, *, t_new):
    nb, n_pages = page_table.shape
    width = (n_pages + 1) * PAGE_SIZE
    page_spec = lambda p: pl.BlockSpec((1, IDX_DIM, PAGE_SIZE), lambda b, pt, p=p: (pt[b, p], 0, 0))
    grid_spec = pltpu.PrefetchScalarGridSpec(
        num_scalar_prefetch=1,
        grid=(nb,),
        in_specs=[pl.BlockSpec((1, t_new * IDX_HEADS, IDX_DIM), lambda b, pt: (b, 0, 0)),
                  pl.BlockSpec((1, t_new * IDX_HEADS, 1), lambda b, pt: (b, 0, 0)),
                  pl.BlockSpec((1, PAGE_SIZE, IDX_DIM), lambda b, pt: (b, 0, 0))]
                 + [page_spec(p) for p in range(n_pages)],
        out_specs=pl.BlockSpec((1, t_new, width), lambda b, pt: (b, 0, 0)),
    )
    return pl.pallas_call(
        functools.partial(_sample_scores_kernel, n_pages, t_new),
        grid_spec=grid_spec,
        out_shape=jax.ShapeDtypeStruct((nb, t_new, width), F32),
        compiler_params=pltpu.CompilerParams(dimension_semantics=("arbitrary",),
                                             vmem_limit_bytes=VMEM_LIMIT),
        name="sample_scores",
    )(page_table, qi_rows, w_col, ki_new_pad, *([cache_kidx_t] * n_pages))


def _sample_select_kernel(topk, sc_ref, sel_ref):
    rows, width = sc_ref.shape

    def count(pred):
        return jnp.sum(pred(sc_ref[...]).astype(F32), axis=1, keepdims=True)

    thr = _kth_largest(lambda cand: count(lambda x: x >= cand), topk, (rows, 1))

    surplus = (count(lambda x: x >= thr) > topk) & (thr > NEG_INF)
    has_ties = jnp.max(surplus.astype(F32)) > 0.5

    @pl.when(jnp.logical_not(has_ties))
    def _():
        sc = sc_ref[...]
        sel_ref[...] = jnp.where((sc >= thr) & (sc > NEG_INF), 1.0, 0.0).astype(F32)

    @pl.when(has_ties)
    def _():
        quota = topk - count(lambda x: x > thr)
        earlier = (lax.broadcasted_iota(I32, (LANES, LANES), 0)
                   < lax.broadcasted_iota(I32, (LANES, LANES), 1)).astype(BF16)
        seen = jnp.zeros((rows, 1), F32)
        for j in range(width // LANES):
            cols = slice(j * LANES, (j + 1) * LANES)
            x = sc_ref[:, cols]
            tie = (x == thr).astype(BF16)
            before = seen + _dot(tie, earlier)
            sel = ((x > thr) | ((x == thr) & (before < quota))) & (x > NEG_INF)
            sel_ref[:, cols] = jnp.where(sel, 1.0, 0.0).astype(F32)
            seen = seen + jnp.sum(tie.astype(F32), axis=1, keepdims=True)


def _sample_select(scores2d, *, topk):
    return pl.pallas_call(
        functools.partial(_sample_select_kernel, topk),
        out_shape=jax.ShapeDtypeStruct(scores2d.shape, F32),
        compiler_params=pltpu.CompilerParams(vmem_limit_bytes=VMEM_LIMIT),
        name="sample_select",
    )(scores2d)


SC_CORES = 2
SC_SUBCORES = 16
SC_LANES = 16
GATHER_ROWS = 128


def _sample_gather(sel2d, page_table, cache_k_rows, cache_v_rows, *, n_sel):
    nq = sel2d.shape[0]
    nb, n_pages = page_table.shape
    past = n_pages * PAGE_SIZE
    t_new = nq // nb
    rows_per_q = n_sel * N_HEADS
    out = jax.ShapeDtypeStruct((nq * rows_per_q, HEAD_DIM), F32)
    assert nb <= SC_CORES * SC_SUBCORES
    mesh = plsc.VectorSubcoreMesh(core_axis_name="c", subcore_axis_name="s",
                                  num_cores=SC_CORES, num_subcores=SC_SUBCORES)

    def body(sel_hbm, pt_hbm, ck_hbm, cv_hbm, kg_hbm, vg_hbm, sel_v, pt_v, key_v, row_v, buf):
        seq_id = lax.axis_index("c") * SC_SUBCORES + lax.axis_index("s")
        lane = lax.iota(I32, SC_LANES)

        @pl.when(seq_id < nb)
        def _():
            pltpu.sync_copy(pt_hbm.at[seq_id], pt_v)
            gather_sequence(seq_id, lane, sel_hbm, pt_v, ck_hbm, cv_hbm, kg_hbm, vg_hbm,
                            sel_v, key_v, row_v, buf)

    def gather_sequence(seq_id, lane, sel_hbm, pt_v, ck_hbm, cv_hbm, kg_hbm, vg_hbm,
                        sel_v, key_v, row_v, buf):
        for t in range(t_new):
            q = seq_id * t_new + t
            pltpu.sync_copy(sel_hbm.at[q, pl.ds(0, past)], sel_v)
            for j in range(key_v.shape[0] // SC_LANES):
                key_v[pl.ds(j * SC_LANES, SC_LANES)] = jnp.zeros((SC_LANES,), I32)

            def compact(j, cnt):
                m = sel_v[pl.ds(j * SC_LANES, SC_LANES)] > 0.5
                key = j * SC_LANES + lane
                page = plsc.load_gather(pt_v, [lax.shift_right_logical(key, 7)])
                phys = page * PAGE_SIZE + (key & (PAGE_SIZE - 1))
                plsc.store_compressed(key_v.at[pl.ds(cnt, SC_LANES)], phys, mask=m)
                return cnt + jnp.sum(m.astype(I32))

            lax.fori_loop(0, past // SC_LANES, compact, jnp.int32(0))

            def expand(i, carry):
                r = i * SC_LANES + lane
                k = plsc.load_gather(key_v, [lax.shift_right_logical(r, 3)])
                row_v[pl.ds(i * SC_LANES, SC_LANES)] = k * N_HEADS + (r & (N_HEADS - 1))
                return carry

            lax.fori_loop(0, rows_per_q // SC_LANES, expand, 0)

            for w in range(rows_per_q // GATHER_ROWS):
                idx = row_v.at[pl.ds(w * GATHER_ROWS, GATHER_ROWS)]
                dst = pl.ds(q * rows_per_q + w * GATHER_ROWS, GATHER_ROWS)
                pltpu.sync_copy(ck_hbm.at[idx], buf)
                pltpu.sync_copy(buf, kg_hbm.at[dst])
                pltpu.sync_copy(cv_hbm.at[idx], buf)
                pltpu.sync_copy(buf, vg_hbm.at[dst])

    return pl.kernel(
        body, out_type=(out, out), mesh=mesh,
        scratch_types=[pltpu.VMEM((past,), F32), pltpu.VMEM((n_pages,), I32),
                       pltpu.VMEM((n_sel + SC_LANES,), I32), pltpu.VMEM((rows_per_q,), I32),
                       pltpu.VMEM((GATHER_ROWS, HEAD_DIM), F32)],
        compiler_params=pltpu.CompilerParams(needs_layout_passes=False),
        name="sample_gather",
    )(sel2d, page_table, cache_k_rows, cache_v_rows)


def _gathered_attn_kernel(t_new, n_sel, past, sel_ref, q_ref, kg_ref, vg_ref, kn_ref, vn_ref, a_ref):
    cols = n_sel * N_HEADS
    row = lax.broadcasted_iota(I32, (N_HEADS, cols), 0)
    col = lax.broadcasted_iota(I32, (N_HEADS, cols), 1)
    own_head = (col & (N_HEADS - 1)) == row
    slot = lax.shift_right_logical(col, 3)
    ncol = t_new * N_HEADS
    rown = lax.broadcasted_iota(I32, (N_HEADS, ncol), 0)
    coln = lax.broadcasted_iota(I32, (N_HEADS, ncol), 1)
    own_head_new = (coln & (N_HEADS - 1)) == rown
    new_key = lax.shift_right_logical(coln, 3)

    for t in range(t_new):
        heads = slice(t * N_HEADS, (t + 1) * N_HEADS)
        q = q_ref[0, heads, :]
        sel_past = sel_ref[0, t:t + 1, :past]
        n_past = jnp.sum(sel_past, axis=1, keepdims=True).astype(I32)
        s = jnp.where(own_head & (slot < n_past), _dot_nt(q, kg_ref[t].astype(BF16)), NEG_INF)

        taken = jnp.zeros((N_HEADS, ncol), jnp.bool_)
        for j in range(t_new):
            taken = taken | ((new_key == j) & (sel_ref[0, t:t + 1, past + j:past + j + 1] > 0.5))
        sn = jnp.where(taken & own_head_new, _dot_nt(q, kn_ref[0]), NEG_INF)

        m = jnp.maximum(jnp.max(s, axis=1, keepdims=True), jnp.max(sn, axis=1, keepdims=True))
        p, pn = jnp.exp(s - m), jnp.exp(sn - m)
        l = jnp.sum(p, axis=1, keepdims=True) + jnp.sum(pn, axis=1, keepdims=True)
        o = _dot(p.astype(BF16), vg_ref[t].astype(BF16)) + _dot(pn.astype(BF16), vn_ref[0])
        a_ref[0, heads, :] = o / l


def _gathered_attention(sel3, q_rows, kg, vg, k_new, v_new, *, t_new, n_sel, past):
    nb, _, width = sel3.shape
    cols = n_sel * N_HEADS
    rows = t_new * N_HEADS
    per_seq = lambda r, n: pl.BlockSpec((1, r, n), lambda b: (b, 0, 0))
    gathered = pl.BlockSpec((t_new, cols, HEAD_DIM), lambda b: (b, 0, 0))
    return pl.pallas_call(
        functools.partial(_gathered_attn_kernel, t_new, n_sel, past),
        grid=(nb,),
        in_specs=[per_seq(t_new, width), per_seq(rows, HEAD_DIM), gathered, gathered,
                  per_seq(rows, HEAD_DIM), per_seq(rows, HEAD_DIM)],
        out_specs=per_seq(rows, HEAD_DIM),
        out_shape=jax.ShapeDtypeStruct((nb, rows, HEAD_DIM), F32),
        compiler_params=pltpu.CompilerParams(dimension_semantics=("arbitrary",),
                                             vmem_limit_bytes=VMEM_LIMIT),
        name="gathered_attn",
    )(sel3, q_rows, kg.reshape(nb * t_new, cols, HEAD_DIM), vg.reshape(nb * t_new, cols, HEAD_DIM),
      k_new, v_new)


def _sample_combine_kernel(t_new, a_ref, ga_ref, gb_ref, ext_ref, x_ref,
                           wp_ref, ps_ref, wo_ref, g_ref, b_ref, y_ref):
    p_t = []
    for t in range(t_new):
        parts = []
        for g, w in enumerate(POOL_WINDOWS):
            cols = slice(g * POOL_GROUP_DIM, (g + 1) * POOL_GROUP_DIM)
            cur = POOL_STATE + t
            win = ext_ref[cur, :, cols]
            for j in range(1, w):
                win = win + ext_ref[cur - j, :, cols]
            pooled = win / float(w) - ext_ref[cur, :, cols]
            parts.append(_dot(pooled.astype(BF16), wp_ref[g]))
        p_t.append(jnp.concatenate(parts, axis=-1) * ps_ref[...])
    y_ref[...] = _gate_out_norm(a_ref[...], ga_ref[...], jnp.concatenate(p_t, axis=0), gb_ref[...],
                                x_ref[...], wo_ref, g_ref, b_ref)


def _sample_combine(a, ga, gb, ext_tm, x2d, w_pool_b, pool_scale, w_out_b, ln_g, ln_b, *, t_new):
    return pl.pallas_call(
        functools.partial(_sample_combine_kernel, t_new),
        out_shape=jax.ShapeDtypeStruct(x2d.shape, F32),
        compiler_params=pltpu.CompilerParams(vmem_limit_bytes=VMEM_LIMIT),
        name="sample_combine",
    )(a, ga, gb, ext_tm, x2d, w_pool_b, pool_scale, w_out_b, ln_g, ln_b)


def kernel(x_prompt, x_sample, cache_k, cache_v, cache_kidx, state_pool, page_table,
           w_in, w_pool, pool_scale, w_out, ln_g, ln_b):
    assert w_in.shape[0] == DEPTH == 1
    batch, seq, _ = x_prompt.shape
    nb, t_new, _ = x_sample.shape
    n_pool, n_pages = cache_k.shape[1], page_table.shape[1]
    past = n_pages * PAGE_SIZE

    w_t = jnp.swapaxes(w_in[0], 0, 1).astype(BF16)
    w_pool_b = w_pool[0].astype(BF16)
    w_out_b = w_out[0].astype(BF16)
    ps, g, b = pool_scale[0][None, :], ln_g[0][None, :], ln_b[0][None, :]

    xp2d = x_prompt.reshape(batch * seq, D_MODEL)
    (q, kf, kb, vf, vb, ga, qi_hm, _, kib, kit, wit, u, gb) = _project(
        xp2d, w_t, jnp.arange(seq), tm=KEY_CHUNK, rows_per_seq=seq, head_major=True)
    a = _prompt_attention(q, qi_hm, wit, kib, kb, vb, batch=batch, seq=seq)
    y_prompt = _combine(a, ga, gb, u, xp2d, w_pool_b, ps, w_out_b, g, b, seq=seq, tm=256)

    rows_s = nb * t_new
    xs2d = x_sample.reshape(rows_s, D_MODEL)
    (qs, kfs, kbs, vfs, vbs, gas, qis, kifs, kibs, _, wits, us, gbs) = _project(
        xs2d, w_t, jnp.tile(past + jnp.arange(t_new), nb), tm=rows_s, rows_per_seq=rows_s,
        head_major=False)

    per_seq = lambda z, r, n: z.reshape(nb, r, n)
    pad_rows = lambda z, n: jnp.pad(z, ((0, 0), (0, n - z.shape[1]), (0, 0)))
    scores = _sample_scores(
        page_table,
        per_seq(qis, t_new * IDX_HEADS, IDX_DIM),
        per_seq(wits.T, t_new * IDX_HEADS, 1),
        pad_rows(per_seq(kibs, t_new, IDX_DIM), PAGE_SIZE),
        jnp.swapaxes(cache_kidx[0], 1, 2),
        t_new=t_new)
    width = scores.shape[-1]
    sel = _sample_select(scores.reshape(rows_s, width), topk=min(TOPK_MAX, (past + t_new) // 4))
    head_rows = t_new * N_HEADS
    n_sel = min(TOPK_MAX, (past + t_new) // 4)
    kg, vg = _sample_gather(sel, page_table,
                            cache_k.reshape(n_pool * PAGE_SIZE * N_HEADS, HEAD_DIM),
                            cache_v.reshape(n_pool * PAGE_SIZE * N_HEADS, HEAD_DIM), n_sel=n_sel)
    a_s = _gathered_attention(
        per_seq(sel, t_new, width), per_seq(qs, head_rows, HEAD_DIM), kg, vg,
        per_seq(kbs, head_rows, HEAD_DIM), per_seq(vbs, head_rows, HEAD_DIM),
        t_new=t_new, n_sel=n_sel, past=past).reshape(rows_s, ATTN_WIDTH)

    u_ext = jnp.concatenate([state_pool[0], us.reshape(nb, t_new, POOL_WIDTH)], axis=1)
    time_major = lambda z: z.reshape(nb, t_new, z.shape[-1]).transpose(1, 0, 2).reshape(rows_s, z.shape[-1])
    y_tm = _sample_combine(time_major(a_s), time_major(gas), time_major(gbs), u_ext.transpose(1, 0, 2),
                           time_major(xs2d), w_pool_b, ps, w_out_b, g, b, t_new=t_new)
    y_sample = y_tm.reshape(t_new, nb, D_MODEL).transpose(1, 0, 2)

    hd = (N_HEADS, HEAD_DIM)
    return (y_prompt.reshape(batch, seq, D_MODEL),
            y_sample,
            kf.reshape(1, batch, seq, *hd), vf.reshape(1, batch, seq, *hd),
            jnp.swapaxes(kit, 1, 2)[None],
            u.reshape(batch, seq, POOL_WIDTH)[None, :, -POOL_STATE:],
            kfs.reshape(1, nb, t_new, *hd), vfs.reshape(1, nb, t_new, *hd),
            kifs.reshape(1, nb, t_new, IDX_DIM),
            u_ext[None, :, -POOL_STATE:])
```

```python
import functools

import numpy as np
import jax
import jax.numpy as jnp
from jax import lax
from jax.experimental import pallas as pl
from jax.experimental.pallas import tpu as pltpu
from jax.experimental.pallas import tpu_sc as plsc

F32 = jnp.float32
BF16 = jnp.bfloat16
I32 = jnp.int32

D_MODEL = 2048
PAGE_SIZE = 128
N_HEADS = 8
HEAD_DIM = 128
ATTN_WIDTH = N_HEADS * HEAD_DIM
ROT_DIM = HEAD_DIM // 4
ROPE_THETA = 500000.0
IDX_HEADS = 16
IDX_DIM = 64
IDX_ROT_DIM = IDX_DIM // 4
TOPK_MAX = 256
POOL_WINDOWS = (2, 4, 8, 16)
N_POOL_GROUPS = len(POOL_WINDOWS)
POOL_WIDTH = D_MODEL - ATTN_WIDTH
POOL_GROUP_DIM = POOL_WIDTH // N_POOL_GROUPS
POOL_STATE = max(POOL_WINDOWS) - 1
IN_SPLITS = (ATTN_WIDTH, ATTN_WIDTH, ATTN_WIDTH, ATTN_WIDTH,
             IDX_HEADS * IDX_DIM, IDX_DIM, IDX_HEADS, POOL_WIDTH, POOL_WIDTH)
IN_COLS = sum(IN_SPLITS)
DEPTH = 1
DEEPNORM_ALPHA = (2 * DEPTH) ** 0.25
LN_EPS = 1e-5
ATTN_SCALE = HEAD_DIM ** -0.5
IDX_SCALE = (IDX_DIM ** -0.5) * (IDX_HEADS ** -0.5)

LANES = 128
SUBLANES = 8
VMEM_LIMIT = 56 * 1024 * 1024

(_OFF_Q, _OFF_K, _OFF_V, _OFF_GA, _OFF_QI, _OFF_KI, _OFF_WI, _OFF_U, _OFF_GB) = (
    [0] + np.cumsum(IN_SPLITS)[:-1].tolist())

NEG_INF = float("-inf")
M_INIT = -1e30
INT_MIN = -2 ** 31
KEY_OF_NEG_INF = -2139095041

KEY_CHUNK = 256
HALF_CHUNK = 128
Q_TILE = 256


def _dot(a, b):
    return jnp.dot(a, b, preferred_element_type=F32)


def _dot_nt(a, b):
    return lax.dot_general(a, b, (((1,), (1,)), ((), ())), preferred_element_type=F32)


def _sigmoid(x):
    return 1.0 / (1.0 + jnp.exp(-x))


def _rope_angles(pos, rot):
    half = rot // 2
    freqs = ROPE_THETA ** (-np.arange(half, dtype=np.float64) * (2.0 / rot))
    ang = np.asarray(pos, np.float64)[:, None] * freqs[None, :]
    return np.cos(ang), np.sin(ang)


def _rope_tables_t(pos, rot):
    cos, sin = _rope_angles(pos, rot)
    return jnp.asarray(np.stack([cos.T, sin.T]), F32)


def _rope_tables(pos, rot, group):
    half = rot // 2
    cos, sin = _rope_angles(pos, rot)
    lane = np.arange(LANES) % group
    sel = lane % half
    cos_l, sin_l = cos[:, sel], sin[:, sel]
    first = (lane < half)[None, :]
    second = ((lane >= half) & (lane < rot))[None, :]
    c = np.where(first | second, cos_l, 1.0)
    s1 = np.where(first, -sin_l, 0.0)
    s2 = np.where(second, sin_l, 0.0)
    return jnp.asarray(np.stack([c, s1, s2]), F32)


def _rope(x, tab_ref, half):
    fwd = pltpu.roll(x, LANES - half, 1)
    bwd = pltpu.roll(x, half, 1)
    return x * tab_ref[0] + fwd * tab_ref[1] + bwd * tab_ref[2]


def _proj_kernel(head_major, x_ref, wt_ref, th_ref, ti_ref, tit_ref,
                 q_ref, kf_ref, kb_ref, vf_ref, vb_ref, ga_ref, qi_ref, kif_ref, kib_ref, kit_ref,
                 wit_ref, u_ref, gb_ref):
    xb = x_ref[...].astype(BF16)

    def seg(off, n):
        return _dot_nt(xb, wt_ref[off:off + n, :])

    def seg_t(off, n):
        return _dot_nt(wt_ref[off:off + n, :], xb)

    qf, kf = seg(_OFF_Q, ATTN_WIDTH), seg(_OFF_K, ATTN_WIDTH)
    for h in range(N_HEADS):
        sl = slice(h * HEAD_DIM, (h + 1) * HEAD_DIM)
        qh = _rope(qf[:, sl], th_ref, ROT_DIM // 2)
        q_ref[:, sl] = (qh * ATTN_SCALE).astype(BF16)
        kh = _rope(kf[:, sl], th_ref, ROT_DIM // 2)
        kf_ref[:, sl] = kh
        kb_ref[:, sl] = kh.astype(BF16)

    v = seg(_OFF_V, ATTN_WIDTH)
    vf_ref[...] = v
    if head_major:
        vb_ref[0] = v.T.astype(BF16)
    else:
        vb_ref[...] = v.astype(BF16)
    ga_ref[...] = seg(_OFF_GA, ATTN_WIDTH)
    u_ref[...] = seg(_OFF_U, POOL_WIDTH)
    gb_ref[...] = seg(_OFF_GB, POOL_WIDTH)

    qif = seg(_OFF_QI, IDX_HEADS * IDX_DIM)
    for j in range(IDX_HEADS // 2):
        r = _rope(qif[:, j * LANES:(j + 1) * LANES], ti_ref, IDX_ROT_DIM // 2).astype(BF16)
        if head_major:
            qi_ref[2 * j] = r[:, :IDX_DIM]
            qi_ref[2 * j + 1] = r[:, IDX_DIM:]
        else:
            qi_ref[:, j * LANES:(j + 1) * LANES] = r

    half = IDX_ROT_DIM // 2
    kt = seg_t(_OFF_KI, IDX_DIM)
    x1, x2 = kt[:half], kt[half:2 * half]
    cos_t, sin_t = tit_ref[0], tit_ref[1]
    kit_ref[0] = jnp.concatenate([x1 * cos_t - x2 * sin_t, x1 * sin_t + x2 * cos_t, kt[2 * half:]], axis=0)
    kn = jnp.concatenate([seg(_OFF_KI, IDX_DIM), jnp.zeros((xb.shape[0], LANES - IDX_DIM), F32)], axis=1)
    ki = _rope(kn, ti_ref, half)[:, :IDX_DIM]
    kif_ref[...] = ki
    kib_ref[...] = ki.astype(BF16)

    wit_ref[...] = seg_t(_OFF_WI, IDX_HEADS) * IDX_SCALE


def _project(x2d, w_t, pos, *, tm, rows_per_seq, head_major):
    rows = x2d.shape[0]
    nb = rows // tm
    pos_blocks = pos.shape[0] // tm
    seq_blocks = rows_per_seq // tm
    tab_h = _rope_tables(pos, ROT_DIM, HEAD_DIM)
    tab_i = _rope_tables(pos, IDX_ROT_DIM, IDX_DIM)
    tab_it = _rope_tables_t(pos, IDX_ROT_DIM)

    row_blk = lambda n: pl.BlockSpec((tm, n), lambda i: (i, 0))
    tab_blk = pl.BlockSpec((3, tm, LANES), lambda i: (0, i % pos_blocks, 0))
    tabt_blk = pl.BlockSpec((2, IDX_ROT_DIM // 2, tm), lambda i: (0, 0, i % pos_blocks))
    wide = lambda dt: jax.ShapeDtypeStruct((rows, ATTN_WIDTH), dt)
    if head_major:
        qi_shape = jax.ShapeDtypeStruct((IDX_HEADS, rows, IDX_DIM), BF16)
        qi_blk = pl.BlockSpec((IDX_HEADS, tm, IDX_DIM), lambda i: (0, i, 0))
        vb_shape = jax.ShapeDtypeStruct((nb, ATTN_WIDTH, tm), BF16)
        vb_blk = pl.BlockSpec((1, ATTN_WIDTH, tm), lambda i: (i, 0, 0))
    else:
        qi_shape = jax.ShapeDtypeStruct((rows, IDX_HEADS * IDX_DIM), BF16)
        qi_blk = row_blk(IDX_HEADS * IDX_DIM)
        vb_shape, vb_blk = wide(BF16), row_blk(ATTN_WIDTH)

    out_shape = (wide(BF16), wide(F32), wide(BF16), wide(F32), vb_shape, wide(F32),
                 qi_shape,
                 jax.ShapeDtypeStruct((rows, IDX_DIM), F32), jax.ShapeDtypeStruct((rows, IDX_DIM), BF16),
                 jax.ShapeDtypeStruct((rows // rows_per_seq, IDX_DIM, rows_per_seq), F32),
                 jax.ShapeDtypeStruct((IDX_HEADS, rows), F32), wide(F32), wide(F32))
    out_specs = (row_blk(ATTN_WIDTH),) * 4 + (vb_blk, row_blk(ATTN_WIDTH)) + (
        qi_blk, row_blk(IDX_DIM), row_blk(IDX_DIM),
        pl.BlockSpec((1, IDX_DIM, tm), lambda i: (i // seq_blocks, 0, i % seq_blocks)),
        pl.BlockSpec((IDX_HEADS, tm), lambda i: (0, i)),
        row_blk(POOL_WIDTH), row_blk(POOL_WIDTH))
    return pl.pallas_call(
        functools.partial(_proj_kernel, head_major),
        grid=(nb,),
        in_specs=[row_blk(D_MODEL),
                  pl.BlockSpec((IN_COLS, D_MODEL), lambda i: (0, 0), pipeline_mode=pl.Buffered(1)),
                  tab_blk, tab_blk, tabt_blk],
        out_specs=out_specs,
        out_shape=out_shape,
        compiler_params=pltpu.CompilerParams(dimension_semantics=("arbitrary",),
                                             vmem_limit_bytes=VMEM_LIMIT),
        name="proj_hm" if head_major else "proj_rm",
    )(x2d, w_t, tab_h, tab_i, tab_it)


def _key_to_float(key):
    return pltpu.bitcast(key ^ ((key >> 31) & 0x7FFFFFFF), F32)


def _kth_largest(count_ge, topk, shape):
    def bit_body(it, prefix):
        trial = prefix | jnp.left_shift(jnp.int32(1), 31 - it)
        cnt = count_ge(_key_to_float(trial ^ INT_MIN))
        return jnp.where(cnt >= topk, trial, prefix)

    prefix = lax.fori_loop(0, 32, bit_body, jnp.zeros(shape, I32))
    return _key_to_float(jnp.maximum(prefix ^ INT_MIN, KEY_OF_NEG_INF))


def _fold_rows(x, op):
    return op(x.reshape(x.shape[0] // SUBLANES, SUBLANES, x.shape[1]), axis=0)


def _prompt_attn_kernel(topk, q_ref, qi_ref, wit_ref, ki_ref, k_ref, vt_ref, a_ref,
                        sc_s, bias_s, acc_s):
    i = pl.program_id(1)
    n_chunks = (i * Q_TILE + Q_TILE + KEY_CHUNK - 1) // KEY_CHUNK
    qpos = i * Q_TILE + lax.broadcasted_iota(I32, (1, Q_TILE), 1)
    row = lax.broadcasted_iota(I32, (HALF_CHUNK, 1), 0)
    halves = [slice(r * HALF_CHUNK, (r + 1) * HALF_CHUNK) for r in range(KEY_CHUNK // HALF_CHUNK)]

    def score_chunk(c, carry):
        for rows in halves:
            kic = ki_ref[c, rows, :]
            acc = jnp.zeros((HALF_CHUNK, Q_TILE), F32)
            for h in range(IDX_HEADS):
                st = _dot_nt(kic, qi_ref[h])
                acc = acc + jnp.maximum(st, 0.0) * wit_ref[h:h + 1, :]
            causal = (c * KEY_CHUNK + rows.start + row) <= qpos
            sc_s[c, rows, :] = jnp.where(causal, acc, NEG_INF)
        return carry

    lax.fori_loop(0, n_chunks, score_chunk, 0)

    def count(pred):
        def body(c, cnt):
            for rows in halves:
                cnt = cnt + _fold_rows(pred(sc_s[c, rows, :]).astype(F32), jnp.sum)
            return cnt
        cnt = lax.fori_loop(0, n_chunks, body, jnp.zeros((SUBLANES, Q_TILE), F32))
        return jnp.sum(cnt, axis=0, keepdims=True)

    thr = _kth_largest(lambda cand: count(lambda x: x >= cand), topk, (1, Q_TILE))

    surplus = (count(lambda x: x >= thr) > topk) & (thr > NEG_INF)
    has_ties = jnp.max(surplus.astype(F32)) > 0.5

    @pl.when(jnp.logical_not(has_ties))
    def _():
        def bias_chunk(c, carry):
            for rows in halves:
                sel = (sc_s[c, rows, :] >= thr) & ((c * KEY_CHUNK + rows.start + row) <= qpos)
                bias_s[c, rows, :] = jnp.where(sel, 0.0, NEG_INF).astype(F32)
            return carry

        lax.fori_loop(0, n_chunks, bias_chunk, 0)

    @pl.when(has_ties)
    def _():
        quota = topk - count(lambda x: x > thr)
        earlier = (lax.broadcasted_iota(I32, (HALF_CHUNK, HALF_CHUNK), 0)
                   > lax.broadcasted_iota(I32, (HALF_CHUNK, HALF_CHUNK), 1)).astype(BF16)

        def bias_chunk(c, seen):
            for rows in halves:
                x = sc_s[c, rows, :]
                tie = (x == thr).astype(BF16)
                before = seen + _dot(earlier, tie)
                sel = ((x > thr) | ((x == thr) & (before < quota))) \
                    & ((c * KEY_CHUNK + rows.start + row) <= qpos)
                bias_s[c, rows, :] = jnp.where(sel, 0.0, NEG_INF).astype(F32)
                seen = seen + jnp.sum(tie.astype(F32), axis=0, keepdims=True)
            return seen

        lax.fori_loop(0, n_chunks, bias_chunk, jnp.zeros((1, Q_TILE), F32))

    acc_s[...] = jnp.zeros(acc_s.shape, F32)
    lane_groups = [slice(g * LANES, (g + 1) * LANES) for g in range(Q_TILE // LANES)]
    n_state = N_HEADS * len(lane_groups)

    def attn_chunk(c, state):
        ms, ls = state
        new_ms, new_ls = [], []
        for h in range(N_HEADS):
            sl = slice(h * HEAD_DIM, (h + 1) * HEAD_DIM)
            for g, qs in enumerate(lane_groups):
                m_old, l_old = ms[h * len(lane_groups) + g], ls[h * len(lane_groups) + g]
                s = _dot_nt(k_ref[c, :, sl], q_ref[qs, sl]) + bias_s[c, :, qs]
                m_new = jnp.maximum(m_old, jnp.max(_fold_rows(s, jnp.max), axis=0, keepdims=True))
                alpha = jnp.exp(m_old - m_new)
                p = jnp.exp(s - m_new)
                new_ms.append(m_new)
                new_ls.append(alpha * l_old + _fold_rows(p, jnp.sum))
                acc_s[h, :, qs] = alpha * acc_s[h, :, qs] + _dot(vt_ref[c, sl, :], p.astype(BF16))
        return tuple(new_ms), tuple(new_ls)

    _, ls = lax.fori_loop(
        0, n_chunks, attn_chunk,
        (tuple(jnp.full((1, LANES), M_INIT, F32) for _ in range(n_state)),
         tuple(jnp.zeros((SUBLANES, LANES), F32) for _ in range(n_state))))

    for h in range(N_HEADS):
        l = jnp.concatenate([jnp.sum(ls[h * len(lane_groups) + g], axis=0, keepdims=True)
                             for g in range(len(lane_groups))], axis=1)
        a_ref[:, h * HEAD_DIM:(h + 1) * HEAD_DIM] = (acc_s[h] / l).T


def _prompt_attention(q, qi_hm, wit, ki_b, k_b, vt3, *, batch, seq):
    nq = seq // Q_TILE
    nch = seq // KEY_CHUNK
    topk = min(TOPK_MAX, seq // 4)
    ki3 = ki_b.reshape(batch * nch, KEY_CHUNK, IDX_DIM)
    k3 = k_b.reshape(batch * nch, KEY_CHUNK, ATTN_WIDTH)
    resident = lambda r, n: pl.BlockSpec((nch, r, n), lambda b, i: (b, 0, 0),
                                         pipeline_mode=pl.Buffered(1))
    return pl.pallas_call(
        functools.partial(_prompt_attn_kernel, topk),
        grid=(batch, nq),
        in_specs=[pl.BlockSpec((Q_TILE, ATTN_WIDTH), lambda b, i: (b * nq + i, 0)),
                  pl.BlockSpec((IDX_HEADS, Q_TILE, IDX_DIM), lambda b, i: (0, b * nq + i, 0)),
                  pl.BlockSpec((IDX_HEADS, Q_TILE), lambda b, i: (0, b * nq + i)),
                  resident(KEY_CHUNK, IDX_DIM), resident(KEY_CHUNK, ATTN_WIDTH),
                  resident(ATTN_WIDTH, KEY_CHUNK)],
        out_specs=pl.BlockSpec((Q_TILE, ATTN_WIDTH), lambda b, i: (b * nq + i, 0)),
        out_shape=jax.ShapeDtypeStruct((batch * seq, ATTN_WIDTH), F32),
        scratch_shapes=[pltpu.VMEM((nch, KEY_CHUNK, Q_TILE), F32),
                        pltpu.VMEM((nch, KEY_CHUNK, Q_TILE), F32),
                        pltpu.VMEM((N_HEADS, HEAD_DIM, Q_TILE), F32)],
        compiler_params=pltpu.CompilerParams(dimension_semantics=("arbitrary", "arbitrary"),
                                             vmem_limit_bytes=VMEM_LIMIT),
        name="prompt_attn",
    )(q, qi_hm, wit, ki3, k3, vt3)


HALO = 16


def _gate_out_norm(a, ga, p, gb, x, wo_ref, g_ref, b_ref):
    mixed = jnp.concatenate([a * (ga * _sigmoid(ga)), p * (gb * _sigmoid(gb))], axis=-1)
    h = DEEPNORM_ALPHA * x + _dot(mixed.astype(BF16), wo_ref[...])
    mu = jnp.mean(h, axis=-1, keepdims=True)
    d = h - mu
    var = jnp.mean(d * d, axis=-1, keepdims=True)
    return d * lax.rsqrt(var + LN_EPS) * g_ref[...] + b_ref[...]


def _combine_kernel(blocks_per_seq, a_ref, ga_ref, gb_ref, u_ref, uh_ref, x_ref,
                    wp_ref, ps_ref, wo_ref, g_ref, b_ref, y_ref, ext_s):
    i = pl.program_id(0)
    tm = u_ref.shape[0]
    first = (i % blocks_per_seq) == 0
    ext_s[0:HALO, :] = jnp.where(first, 0.0, uh_ref[...])
    ext_s[HALO:HALO + tm, :] = u_ref[...]
    pos = (i % blocks_per_seq) * tm + lax.broadcasted_iota(I32, (tm, 1), 0)

    parts = []
    for g, w in enumerate(POOL_WINDOWS):
        cols = slice(g * POOL_GROUP_DIM, (g + 1) * POOL_GROUP_DIM)
        win = ext_s[HALO:HALO + tm, cols]
        for j in range(1, w):
            win = win + ext_s[HALO - j:HALO - j + tm, cols]
        count = jnp.minimum(pos + 1, w).astype(F32)
        pooled = win / count - u_ref[:, cols]
        parts.append(_dot(pooled.astype(BF16), wp_ref[g]))
    p = jnp.concatenate(parts, axis=-1) * ps_ref[...]
    y_ref[...] = _gate_out_norm(a_ref[...], ga_ref[...], p, gb_ref[...], x_ref[...],
                                wo_ref, g_ref, b_ref)


def _combine(a, ga, gb, u, x2d, w_pool_b, pool_scale, w_out_b, ln_g, ln_b, *, seq, tm):
    rows = a.shape[0]
    nb = rows // tm
    bps = seq // tm
    hpb = tm // HALO
    row_blk = lambda n: pl.BlockSpec((tm, n), lambda i: (i, 0))
    const = lambda shape: pl.BlockSpec(shape, lambda i: (0,) * len(shape))
    return pl.pallas_call(
        functools.partial(_combine_kernel, bps),
        grid=(nb,),
        in_specs=[row_blk(ATTN_WIDTH), row_blk(ATTN_WIDTH), row_blk(POOL_WIDTH), row_blk(POOL_WIDTH),
                  pl.BlockSpec((HALO, POOL_WIDTH), lambda i: (jnp.maximum(i * hpb - 1, 0), 0)),
                  row_blk(D_MODEL),
                  const((N_POOL_GROUPS, POOL_GROUP_DIM, POOL_GROUP_DIM)), const((1, POOL_WIDTH)),
                  const((D_MODEL, D_MODEL)), const((1, D_MODEL)), const((1, D_MODEL))],
        out_specs=row_blk(D_MODEL),
        out_shape=jax.ShapeDtypeStruct((rows, D_MODEL), F32),
        scratch_shapes=[pltpu.VMEM((HALO + tm, POOL_WIDTH), F32)],
        compiler_params=pltpu.CompilerParams(dimension_semantics=("arbitrary",),
                                             vmem_limit_bytes=VMEM_LIMIT),
        name="combine",
    )(a, ga, gb, u, u, x2d, w_pool_b, pool_scale, w_out_b, ln_g, ln_b)


def _sample_scores_kernel(n_pages, t_new, pt_ref, qi_ref, w_ref, kn_ref, *refs):
    page_refs, sc_ref = refs[:n_pages], refs[n_pages]
    qi = qi_ref[0]
    w = w_ref[0]

    def head_sum(s):
        r = jnp.maximum(s, 0.0) * w
        return jnp.sum(r.reshape(t_new, IDX_HEADS, s.shape[-1]), axis=1)

    for p in range(n_pages):
        kp_t = page_refs[p][0].astype(BF16)
        sc_ref[0, :, p * PAGE_SIZE:(p + 1) * PAGE_SIZE] = head_sum(_dot(qi, kp_t))
    s_new = head_sum(_dot_nt(qi, kn_ref[0]))
    t_idx = lax.broadcasted_iota(I32, s_new.shape, 0)
    j_idx = lax.broadcasted_iota(I32, s_new.shape, 1)
    sc_ref[0, :, n_pages * PAGE_SIZE:] = jnp.where(j_idx <= t_idx, s_new, NEG_INF)


def _sample_scores(page_table, qi_rows, w_col, ki_new_pad, cache_kidx_t, *, t_new):
    nb, n_pages = page_table.shape
    width = (n_pages + 1) * PAGE_SIZE
    page_spec = lambda p: pl.BlockSpec((1, IDX_DIM, PAGE_SIZE), lambda b, pt, p=p: (pt[b, p], 0, 0))
    grid_spec = pltpu.PrefetchScalarGridSpec(
        num_scalar_prefetch=1,
        grid=(nb,),
        in_specs=[pl.BlockSpec((1, t_new * IDX_HEADS, IDX_DIM), lambda b, pt: (b, 0, 0)),
                  pl.BlockSpec((1, t_new * IDX_HEADS, 1), lambda b, pt: (b, 0, 0)),
                  pl.BlockSpec((1, PAGE_SIZE, IDX_DIM), lambda b, pt: (b, 0, 0))]
                 + [page_spec(p) for p in range(n_pages)],
        out_specs=pl.BlockSpec((1, t_new, width), lambda b, pt: (b, 0, 0)),
    )
    return pl.pallas_call(
        functools.partial(_sample_scores_kernel, n_pages, t_new),
        grid_spec=grid_spec,
        out_shape=jax.ShapeDtypeStruct((nb, t_new, width), F32),
        compiler_params=pltpu.CompilerParams(dimension_semantics=("arbitrary",),
                                             vmem_limit_bytes=VMEM_LIMIT),
        name="sample_scores",
    )(page_table, qi_rows, w_col, ki_new_pad, *([cache_kidx_t] * n_pages))


def _sample_select_kernel(topk, sc_ref, sel_ref):
    rows, width = sc_ref.shape

    def count(pred):
        return jnp.sum(pred(sc_ref[...]).astype(F32), axis=1, keepdims=True)

    thr = _kth_largest(lambda cand: count(lambda x: x >= cand), topk, (rows, 1))

    surplus = (count(lambda x: x >= thr) > topk) & (thr > NEG_INF)
    has_ties = jnp.max(surplus.astype(F32)) > 0.5

    @pl.when(jnp.logical_not(has_ties))
    def _():
        sc = sc_ref[...]
        sel_ref[...] = jnp.where((sc >= thr) & (sc > NEG_INF), 1.0, 0.0).astype(F32)

    @pl.when(has_ties)
    def _():
        quota = topk - count(lambda x: x > thr)
        earlier = (lax.broadcasted_iota(I32, (LANES, LANES), 0)
                   < lax.broadcasted_iota(I32, (LANES, LANES), 1)).astype(BF16)
        seen = jnp.zeros((rows, 1), F32)
        for j in range(width // LANES):
            cols = slice(j * LANES, (j + 1) * LANES)
            x = sc_ref[:, cols]
            tie = (x == thr).astype(BF16)
            before = seen + _dot(tie, earlier)
            sel = ((x > thr) | ((x == thr) & (before < quota))) & (x > NEG_INF)
            sel_ref[:, cols] = jnp.where(sel, 1.0, 0.0).astype(F32)
            seen = seen + jnp.sum(tie.astype(F32), axis=1, keepdims=True)


def _sample_select(scores2d, *, topk):
    return pl.pallas_call(
        functools.partial(_sample_select_kernel, topk),
        out_shape=jax.ShapeDtypeStruct(scores2d.shape, F32),
        compiler_params=pltpu.CompilerParams(vmem_limit_bytes=VMEM_LIMIT),
        name="sample_select",
    )(scores2d)


SC_CORES = 2
SC_SUBCORES = 16
SC_LANES = 16
GATHER_ROWS = 128


def _sample_gather(sel2d, page_table, cache_k_rows, cache_v_rows, *, n_sel):
    nq = sel2d.shape[0]
    nb, n_pages = page_table.shape
    past = n_pages * PAGE_SIZE
    t_new = nq // nb
    rows_per_q = n_sel * N_HEADS
    out = jax.ShapeDtypeStruct((nq * rows_per_q, HEAD_DIM), F32)
    assert nb <= SC_CORES * SC_SUBCORES
    mesh = plsc.VectorSubcoreMesh(core_axis_name="c", subcore_axis_name="s",
                                  num_cores=SC_CORES, num_subcores=SC_SUBCORES)

    def body(sel_hbm, pt_hbm, ck_hbm, cv_hbm, kg_hbm, vg_hbm, sel_v, pt_v, key_v, row_v, buf):
        seq_id = lax.axis_index("c") * SC_SUBCORES + lax.axis_index("s")
        lane = lax.iota(I32, SC_LANES)

        @pl.when(seq_id < nb)
        def _():
            pltpu.sync_copy(pt_hbm.at[seq_id], pt_v)
            gather_sequence(seq_id, lane, sel_hbm, pt_v, ck_hbm, cv_hbm, kg_hbm, vg_hbm,
                            sel_v, key_v, row_v, buf)

    def gather_sequence(seq_id, lane, sel_hbm, pt_v, ck_hbm, cv_hbm, kg_hbm, vg_hbm,
                        sel_v, key_v, row_v, buf):
        for t in range(t_new):
            q = seq_id * t_new + t
            pltpu.sync_copy(sel_hbm.at[q, pl.ds(0, past)], sel_v)
            for j in range(key_v.shape[0] // SC_LANES):
                key_v[pl.ds(j * SC_LANES, SC_LANES)] = jnp.zeros((SC_LANES,), I32)

            def compact(j, cnt):
                m = sel_v[pl.ds(j * SC_LANES, SC_LANES)] > 0.5
                key = j * SC_LANES + lane
                page = plsc.load_gather(pt_v, [lax.shift_right_logical(key, 7)])
                phys = page * PAGE_SIZE + (key & (PAGE_SIZE - 1))
                plsc.store_compressed(key_v.at[pl.ds(cnt, SC_LANES)], phys, mask=m)
                return cnt + jnp.sum(m.astype(I32))

            lax.fori_loop(0, past // SC_LANES, compact, jnp.int32(0))

            def expand(i, carry):
                r = i * SC_LANES + lane
                k = plsc.load_gather(key_v, [lax.shift_right_logical(r, 3)])
                row_v[pl.ds(i * SC_LANES, SC_LANES)] = k * N_HEADS + (r & (N_HEADS - 1))
                return carry

            lax.fori_loop(0, rows_per_q // SC_LANES, expand, 0)

            for w in range(rows_per_q // GATHER_ROWS):
                idx = row_v.at[pl.ds(w * GATHER_ROWS, GATHER_ROWS)]
                dst = pl.ds(q * rows_per_q + w * GATHER_ROWS, GATHER_ROWS)
                pltpu.sync_copy(ck_hbm.at[idx], buf)
                pltpu.sync_copy(buf, kg_hbm.at[dst])
                pltpu.sync_copy(cv_hbm.at[idx], buf)
                pltpu.sync_copy(buf, vg_hbm.at[dst])

    return pl.kernel(
        body, out_type=(out, out), mesh=mesh,
        scratch_types=[pltpu.VMEM((past,), F32), pltpu.VMEM((n_pages,), I32),
                       pltpu.VMEM((n_sel + SC_LANES,), I32), pltpu.VMEM((rows_per_q,), I32),
                       pltpu.VMEM((GATHER_ROWS, HEAD_DIM), F32)],
        compiler_params=pltpu.CompilerParams(needs_layout_passes=False),
        name="sample_gather",
    )(sel2d, page_table, cache_k_rows, cache_v_rows)


def _gathered_attn_kernel(t_new, n_sel, past, sel_ref, q_ref, kg_ref, vg_ref, kn_ref, vn_ref, a_ref):
    cols = n_sel * N_HEADS
    row = lax.broadcasted_iota(I32, (N_HEADS, cols), 0)
    col = lax.broadcasted_iota(I32, (N_HEADS, cols), 1)
    own_head = (col & (N_HEADS - 1)) == row
    slot = lax.shift_right_logical(col, 3)
    ncol = t_new * N_HEADS
    rown = lax.broadcasted_iota(I32, (N_HEADS, ncol), 0)
    coln = lax.broadcasted_iota(I32, (N_HEADS, ncol), 1)
    own_head_new = (coln & (N_HEADS - 1)) == rown
    new_key = lax.shift_right_logical(coln, 3)

    for t in range(t_new):
        heads = slice(t * N_HEADS, (t + 1) * N_HEADS)
        q = q_ref[0, heads, :]
        sel_past = sel_ref[0, t:t + 1, :past]
        n_past = jnp.sum(sel_past, axis=1, keepdims=True).astype(I32)
        s = jnp.where(own_head & (slot < n_past), _dot_nt(q, kg_ref[t].astype(BF16)), NEG_INF)

        taken = jnp.zeros((N_HEADS, ncol), jnp.bool_)
        for j in range(t_new):
            taken = taken | ((new_key == j) & (sel_ref[0, t:t + 1, past + j:past + j + 1] > 0.5))
        sn = jnp.where(taken & own_head_new, _dot_nt(q, kn_ref[0]), NEG_INF)

        m = jnp.maximum(jnp.max(s, axis=1, keepdims=True), jnp.max(sn, axis=1, keepdims=True))
        p, pn = jnp.exp(s - m), jnp.exp(sn - m)
        l = jnp.sum(p, axis=1, keepdims=True) + jnp.sum(pn, axis=1, keepdims=True)
        o = _dot(p.astype(BF16), vg_ref[t].astype(BF16)) + _dot(pn.astype(BF16), vn_ref[0])
        a_ref[0, heads, :] = o / l


def _gathered_attention(sel3, q_rows, kg, vg, k_new, v_new, *, t_new, n_sel, past):
    nb, _, width = sel3.shape
    cols = n_sel * N_HEADS
    rows = t_new * N_HEADS
    per_seq = lambda r, n: pl.BlockSpec((1, r, n), lambda b: (b, 0, 0))
    gathered = pl.BlockSpec((t_new, cols, HEAD_DIM), lambda b: (b, 0, 0))
    return pl.pallas_call(
        functools.partial(_gathered_attn_kernel, t_new, n_sel, past),
        grid=(nb,),
        in_specs=[per_seq(t_new, width), per_seq(rows, HEAD_DIM), gathered, gathered,
                  per_seq(rows, HEAD_DIM), per_seq(rows, HEAD_DIM)],
        out_specs=per_seq(rows, HEAD_DIM),
        out_shape=jax.ShapeDtypeStruct((nb, rows, HEAD_DIM), F32),
        compiler_params=pltpu.CompilerParams(dimension_semantics=("arbitrary",),
                                             vmem_limit_bytes=VMEM_LIMIT),
        name="gathered_attn",
    )(sel3, q_rows, kg.reshape(nb * t_new, cols, HEAD_DIM), vg.reshape(nb * t_new, cols, HEAD_DIM),
      k_new, v_new)


def _sample_combine_kernel(t_new, a_ref, ga_ref, gb_ref, ext_ref, x_ref,
                           wp_ref, ps_ref, wo_ref, g_ref, b_ref, y_ref):
    p_t = []
    for t in range(t_new):
        parts = []
        for g, w in enumerate(POOL_WINDOWS):
            cols = slice(g * POOL_GROUP_DIM, (g + 1) * POOL_GROUP_DIM)
            cur = POOL_STATE + t
            win = ext_ref[cur, :, cols]
            for j in range(1, w):
                win = win + ext_ref[cur - j, :, cols]
            pooled = win / float(w) - ext_ref[cur, :, cols]
            parts.append(_dot(pooled.astype(BF16), wp_ref[g]))
        p_t.append(jnp.concatenate(parts, axis=-1) * ps_ref[...])
    y_ref[...] = _gate_out_norm(a_ref[...], ga_ref[...], jnp.concatenate(p_t, axis=0), gb_ref[...],
                                x_ref[...], wo_ref, g_ref, b_ref)


def _sample_combine(a, ga, gb, ext_tm, x2d, w_pool_b, pool_scale, w_out_b, ln_g, ln_b, *, t_new):
    return pl.pallas_call(
        functools.partial(_sample_combine_kernel, t_new),
        out_shape=jax.ShapeDtypeStruct(x2d.shape, F32),
        compiler_params=pltpu.CompilerParams(vmem_limit_bytes=VMEM_LIMIT),
        name="sample_combine",
    )(a, ga, gb, ext_tm, x2d, w_pool_b, pool_scale, w_out_b, ln_g, ln_b)


def kernel(x_prompt, x_sample, cache_k, cache_v, cache_kidx, state_pool, page_table,
           w_in, w_pool, pool_scale, w_out, ln_g, ln_b):
    assert w_in.shape[0] == DEPTH == 1
    batch, seq, _ = x_prompt.shape
    nb, t_new, _ = x_sample.shape
    n_pool, n_pages = cache_k.shape[1], page_table.shape[1]
    past = n_pages * PAGE_SIZE

    w_t = jnp.swapaxes(w_in[0], 0, 1).astype(BF16)
    w_pool_b = w_pool[0].astype(BF16)
    w_out_b = w_out[0].astype(BF16)
    ps, g, b = pool_scale[0][None, :], ln_g[0][None, :], ln_b[0][None, :]

    xp2d = x_prompt.reshape(batch * seq, D_MODEL)
    (q, kf, kb, vf, vb, ga, qi_hm, _, kib, kit, wit, u, gb) = _project(
        xp2d, w_t, np.arange(seq), tm=KEY_CHUNK, rows_per_seq=seq, head_major=True)
    a = _prompt_attention(q, qi_hm, wit, kib, kb, vb, batch=batch, seq=seq)
    y_prompt = _combine(a, ga, gb, u, xp2d, w_pool_b, ps, w_out_b, g, b, seq=seq, tm=256)

    rows_s = nb * t_new
    xs2d = x_sample.reshape(rows_s, D_MODEL)
    (qs, kfs, kbs, vfs, vbs, gas, qis, kifs, kibs, _, wits, us, gbs) = _project(
        xs2d, w_t, np.tile(past + np.arange(t_new), nb), tm=rows_s, rows_per_seq=rows_s,
        head_major=False)

    per_seq = lambda z, r, n: z.reshape(nb, r, n)
    pad_rows = lambda z, n: jnp.pad(z, ((0, 0), (0, n - z.shape[1]), (0, 0)))
    scores = _sample_scores(
        page_table,
        per_seq(qis, t_new * IDX_HEADS, IDX_DIM),
        per_seq(wits.T, t_new * IDX_HEADS, 1),
        pad_rows(per_seq(kibs, t_new, IDX_DIM), PAGE_SIZE),
        jnp.swapaxes(cache_kidx[0], 1, 2),
        t_new=t_new)
    width = scores.shape[-1]
    sel = _sample_select(scores.reshape(rows_s, width), topk=min(TOPK_MAX, (past + t_new) // 4))
    head_rows = t_new * N_HEADS
    n_sel = min(TOPK_MAX, (past + t_new) // 4)
    kg, vg = _sample_gather(sel, page_table,
                            cache_k.reshape(n_pool * PAGE_SIZE * N_HEADS, HEAD_DIM),
                            cache_v.reshape(n_pool * PAGE_SIZE * N_HEADS, HEAD_DIM), n_sel=n_sel)
    a_s = _gathered_attention(
        per_seq(sel, t_new, width), per_seq(qs, head_rows, HEAD_DIM), kg, vg,
        per_seq(kbs, head_rows, HEAD_DIM), per_seq(vbs, head_rows, HEAD_DIM),
        t_new=t_new, n_sel=n_sel, past=past).reshape(rows_s, ATTN_WIDTH)

    u_ext = jnp.concatenate([state_pool[0], us.reshape(nb, t_new, POOL_WIDTH)], axis=1)
    time_major = lambda z: z.reshape(nb, t_new, z.shape[-1]).transpose(1, 0, 2).reshape(rows_s, z.shape[-1])
    y_tm = _sample_combine(time_major(a_s), time_major(gas), time_major(gbs), u_ext.transpose(1, 0, 2),
                           time_major(xs2d), w_pool_b, ps, w_out_b, g, b, t_new=t_new)
    y_sample = y_tm.reshape(t_new, nb, D_MODEL).transpose(1, 0, 2)

    hd = (N_HEADS, HEAD_DIM)
    return (y_prompt.reshape(batch, seq, D_MODEL),
            y_sample,
            kf.reshape(1, batch, seq, *hd), vf.reshape(1, batch, seq, *hd),
            jnp.swapaxes(kit, 1, 2)[None],
            u.reshape(batch, seq, POOL_WIDTH)[None, :, -POOL_STATE:],
            kfs.reshape(1, nb, t_new, *hd), vfs.reshape(1, nb, t_new, *hd),
            kifs.reshape(1, nb, t_new, IDX_DIM),
            u_ext[None, :, -POOL_STATE:])
```

```python
import functools

import numpy as np
import jax
import jax.numpy as jnp
from jax import lax
from jax.experimental import pallas as pl
from jax.experimental.pallas import tpu as pltpu
from jax.experimental.pallas import tpu_sc as plsc

F32 = jnp.float32
BF16 = jnp.bfloat16
I32 = jnp.int32

D_MODEL = 2048
PAGE_SIZE = 128
N_HEADS = 8
HEAD_DIM = 128
ATTN_WIDTH = N_HEADS * HEAD_DIM
ROT_DIM = HEAD_DIM // 4
ROPE_THETA = 500000.0
IDX_HEADS = 16
IDX_DIM = 64
IDX_ROT_DIM = IDX_DIM // 4
TOPK_MAX = 256
POOL_WINDOWS = (2, 4, 8, 16)
N_POOL_GROUPS = len(POOL_WINDOWS)
POOL_WIDTH = D_MODEL - ATTN_WIDTH
POOL_GROUP_DIM = POOL_WIDTH // N_POOL_GROUPS
POOL_STATE = max(POOL_WINDOWS) - 1
IN_SPLITS = (ATTN_WIDTH, ATTN_WIDTH, ATTN_WIDTH, ATTN_WIDTH,
             IDX_HEADS * IDX_DIM, IDX_DIM, IDX_HEADS, POOL_WIDTH, POOL_WIDTH)
IN_COLS = sum(IN_SPLITS)
DEPTH = 1
DEEPNORM_ALPHA = (2 * DEPTH) ** 0.25
LN_EPS = 1e-5
ATTN_SCALE = HEAD_DIM ** -0.5
IDX_SCALE = (IDX_DIM ** -0.5) * (IDX_HEADS ** -0.5)

LANES = 128
SUBLANES = 8
VMEM_LIMIT = 56 * 1024 * 1024

(_OFF_Q, _OFF_K, _OFF_V, _OFF_GA, _OFF_QI, _OFF_KI, _OFF_WI, _OFF_U, _OFF_GB) = (
    [0] + np.cumsum(IN_SPLITS)[:-1].tolist())

NEG_INF = float("-inf")
M_INIT = -1e30
INT_MIN = -2 ** 31
KEY_OF_NEG_INF = -2139095041

KEY_CHUNK = 256
HALF_CHUNK = 128
Q_TILE = 256


def _dot(a, b):
    return jnp.dot(a, b, preferred_element_type=F32)


def _dot_nt(a, b):
    return lax.dot_general(a, b, (((1,), (1,)), ((), ())), preferred_element_type=F32)


def _sigmoid(x):
    return 1.0 / (1.0 + jnp.exp(-x))


def _rope_angles(pos, rot):
    half = rot // 2
    freqs = ROPE_THETA ** (-np.arange(half, dtype=np.float64) * (2.0 / rot))
    ang = np.asarray(pos, np.float64)[:, None] * freqs[None, :]
    return np.cos(ang), np.sin(ang)


def _rope_tables_t(pos, rot):
    cos, sin = _rope_angles(pos, rot)
    return jnp.asarray(np.stack([cos.T, sin.T]), F32)


def _rope_tables(pos, rot, group):
    half = rot // 2
    cos, sin = _rope_angles(pos, rot)
    lane = np.arange(LANES) % group
    sel = lane % half
    cos_l, sin_l = cos[:, sel], sin[:, sel]
    first = (lane < half)[None, :]
    second = ((lane >= half) & (lane < rot))[None, :]
    c = np.where(first | second, cos_l, 1.0)
    s1 = np.where(first, -sin_l, 0.0)
    s2 = np.where(second, sin_l, 0.0)
    return jnp.asarray(np.stack([c, s1, s2]), F32)


def _rope(x, tab_ref, half):
    fwd = pltpu.roll(x, LANES - half, 1)
    bwd = pltpu.roll(x, half, 1)
    return x * tab_ref[0] + fwd * tab_ref[1] + bwd * tab_ref[2]


def _proj_kernel(head_major, x_ref, wt_ref, th_ref, ti_ref, tit_ref,
                 q_ref, kf_ref, kb_ref, vf_ref, vb_ref, ga_ref, qi_ref, kif_ref, kib_ref, kit_ref,
                 wit_ref, u_ref, gb_ref):
    xb = x_ref[...].astype(BF16)

    def seg(off, n):
        return _dot_nt(xb, wt_ref[off:off + n, :])

    def seg_t(off, n):
        return _dot_nt(wt_ref[off:off + n, :], xb)

    qf, kf = seg(_OFF_Q, ATTN_WIDTH), seg(_OFF_K, ATTN_WIDTH)
    for h in range(N_HEADS):
        sl = slice(h * HEAD_DIM, (h + 1) * HEAD_DIM)
        qh = _rope(qf[:, sl], th_ref, ROT_DIM // 2)
        q_ref[:, sl] = (qh * ATTN_SCALE).astype(BF16)
        kh = _rope(kf[:, sl], th_ref, ROT_DIM // 2)
        kf_ref[:, sl] = kh
        kb_ref[:, sl] = kh.astype(BF16)

    v = seg(_OFF_V, ATTN_WIDTH)
    vf_ref[...] = v
    if head_major:
        vb_ref[0] = v.T.astype(BF16)
    else:
        vb_ref[...] = v.astype(BF16)
    ga_ref[...] = seg(_OFF_GA, ATTN_WIDTH)
    u_ref[...] = seg(_OFF_U, POOL_WIDTH)
    gb_ref[...] = seg(_OFF_GB, POOL_WIDTH)

    qif = seg(_OFF_QI, IDX_HEADS * IDX_DIM)
    for j in range(IDX_HEADS // 2):
        r = _rope(qif[:, j * LANES:(j + 1) * LANES], ti_ref, IDX_ROT_DIM // 2).astype(BF16)
        if head_major:
            qi_ref[2 * j] = r[:, :IDX_DIM]
            qi_ref[2 * j + 1] = r[:, IDX_DIM:]
        else:
            qi_ref[:, j * LANES:(j + 1) * LANES] = r

    half = IDX_ROT_DIM // 2
    kt = seg_t(_OFF_KI, IDX_DIM)
    x1, x2 = kt[:half], kt[half:2 * half]
    cos_t, sin_t = tit_ref[0], tit_ref[1]
    kit_ref[0] = jnp.concatenate([x1 * cos_t - x2 * sin_t, x1 * sin_t + x2 * cos_t, kt[2 * half:]], axis=0)
    kn = jnp.concatenate([seg(_OFF_KI, IDX_DIM), jnp.zeros((xb.shape[0], LANES - IDX_DIM), F32)], axis=1)
    ki = _rope(kn, ti_ref, half)[:, :IDX_DIM]
    kif_ref[...] = ki
    kib_ref[...] = ki.astype(BF16)

    wit_ref[...] = seg_t(_OFF_WI, IDX_HEADS) * IDX_SCALE


def _project(x2d, w_t, pos, *, tm, rows_per_seq, head_major):
    rows = x2d.shape[0]
    nb = rows // tm
    pos_blocks = pos.shape[0] // tm
    seq_blocks = rows_per_seq // tm
    tab_h = _rope_tables(pos, ROT_DIM, HEAD_DIM)
    tab_i = _rope_tables(pos, IDX_ROT_DIM, IDX_DIM)
    tab_it = _rope_tables_t(pos, IDX_ROT_DIM)

    row_blk = lambda n: pl.BlockSpec((tm, n), lambda i: (i, 0))
    tab_blk = pl.BlockSpec((3, tm, LANES), lambda i: (0, i % pos_blocks, 0))
    tabt_blk = pl.BlockSpec((2, IDX_ROT_DIM // 2, tm), lambda i: (0, 0, i % pos_blocks))
    wide = lambda dt: jax.ShapeDtypeStruct((rows, ATTN_WIDTH), dt)
    if head_major:
        qi_shape = jax.ShapeDtypeStruct((IDX_HEADS, rows, IDX_DIM), BF16)
        qi_blk = pl.BlockSpec((IDX_HEADS, tm, IDX_DIM), lambda i: (0, i, 0))
        vb_shape = jax.ShapeDtypeStruct((nb, ATTN_WIDTH, tm), BF16)
        vb_blk = pl.BlockSpec((1, ATTN_WIDTH, tm), lambda i: (i, 0, 0))
    else:
        qi_shape = jax.ShapeDtypeStruct((rows, IDX_HEADS * IDX_DIM), BF16)
        qi_blk = row_blk(IDX_HEADS * IDX_DIM)
        vb_shape, vb_blk = wide(BF16), row_blk(ATTN_WIDTH)

    out_shape = (wide(BF16), wide(F32), wide(BF16), wide(F32), vb_shape, wide(F32),
                 qi_shape,
                 jax.ShapeDtypeStruct((rows, IDX_DIM), F32), jax.ShapeDtypeStruct((rows, IDX_DIM), BF16),
                 jax.ShapeDtypeStruct((rows // rows_per_seq, IDX_DIM, rows_per_seq), F32),
                 jax.ShapeDtypeStruct((IDX_HEADS, rows), F32), wide(F32), wide(F32))
    out_specs = (row_blk(ATTN_WIDTH),) * 4 + (vb_blk, row_blk(ATTN_WIDTH)) + (
        qi_blk, row_blk(IDX_DIM), row_blk(IDX_DIM),
        pl.BlockSpec((1, IDX_DIM, tm), lambda i: (i // seq_blocks, 0, i % seq_blocks)),
        pl.BlockSpec((IDX_HEADS, tm), lambda i: (0, i)),
        row_blk(POOL_WIDTH), row_blk(POOL_WIDTH))
    return pl.pallas_call(
        functools.partial(_proj_kernel, head_major),
        grid=(nb,),
        in_specs=[row_blk(D_MODEL),
                  pl.BlockSpec((IN_COLS, D_MODEL), lambda i: (0, 0), pipeline_mode=pl.Buffered(1)),
                  tab_blk, tab_blk, tabt_blk],
        out_specs=out_specs,
        out_shape=out_shape,
        compiler_params=pltpu.CompilerParams(dimension_semantics=("arbitrary",),
                                             vmem_limit_bytes=VMEM_LIMIT),
        name="proj_hm" if head_major else "proj_rm",
    )(x2d, w_t, tab_h, tab_i, tab_it)


def _key_to_float(key):
    return pltpu.bitcast(key ^ ((key >> 31) & 0x7FFFFFFF), F32)


def _kth_largest(count_ge, topk, shape):
    def bit_body(it, carry):
        prefix, cnt_at = carry
        trial = prefix | jnp.left_shift(jnp.int32(1), 31 - it)
        cnt = count_ge(_key_to_float(trial ^ INT_MIN))
        take = cnt >= topk
        return jnp.where(take, trial, prefix), jnp.where(take, cnt, cnt_at)

    prefix, cnt_at = lax.fori_loop(0, 32, bit_body, (jnp.zeros(shape, I32), jnp.zeros(shape, F32)))
    return _key_to_float(jnp.maximum(prefix ^ INT_MIN, KEY_OF_NEG_INF)), cnt_at


def _fold_rows(x, op):
    return op(x.reshape(x.shape[0] // SUBLANES, SUBLANES, x.shape[1]), axis=0)


def _prompt_attn_kernel(topk, q_ref, qi_ref, wit_ref, ki_ref, k_ref, vt_ref, a_ref,
                        sc_s, bias_s, acc_s):
    i = pl.program_id(1)
    n_chunks = (i * Q_TILE + Q_TILE + KEY_CHUNK - 1) // KEY_CHUNK
    qpos = i * Q_TILE + lax.broadcasted_iota(I32, (1, Q_TILE), 1)
    row = lax.broadcasted_iota(I32, (HALF_CHUNK, 1), 0)
    halves = [slice(r * HALF_CHUNK, (r + 1) * HALF_CHUNK) for r in range(KEY_CHUNK // HALF_CHUNK)]

    def score_chunk(c, carry):
        for rows in halves:
            kic = ki_ref[c, rows, :]
            acc = jnp.zeros((HALF_CHUNK, Q_TILE), F32)
            for h in range(IDX_HEADS):
                st = _dot_nt(kic, qi_ref[h])
                acc = acc + jnp.maximum(st, 0.0) * wit_ref[h:h + 1, :]
            causal = (c * KEY_CHUNK + rows.start + row) <= qpos
            sc_s[c, rows, :] = jnp.where(causal, acc, NEG_INF)
        return carry

    lax.fori_loop(0, n_chunks, score_chunk, 0)

    def count(pred):
        def body(c, cnt):
            for rows in halves:
                cnt = cnt + _fold_rows(pred(sc_s[c, rows, :]).astype(F32), jnp.sum)
            return cnt
        cnt = lax.fori_loop(0, n_chunks, body, jnp.zeros((SUBLANES, Q_TILE), F32))
        return jnp.sum(cnt, axis=0, keepdims=True)

    thr, n_ge_thr = _kth_largest(lambda cand: count(lambda x: x >= cand), topk, (1, Q_TILE))

    surplus = (n_ge_thr > topk) & (thr > NEG_INF)
    has_ties = jnp.max(surplus.astype(F32)) > 0.5

    @pl.when(jnp.logical_not(has_ties))
    def _():
        def bias_chunk(c, carry):
            for rows in halves:
                sel = (sc_s[c, rows, :] >= thr) & ((c * KEY_CHUNK + rows.start + row) <= qpos)
                bias_s[c, rows, :] = jnp.where(sel, 0.0, NEG_INF).astype(F32)
            return carry

        lax.fori_loop(0, n_chunks, bias_chunk, 0)

    @pl.when(has_ties)
    def _():
        quota = topk - count(lambda x: x > thr)
        earlier = (lax.broadcasted_iota(I32, (HALF_CHUNK, HALF_CHUNK), 0)
                   > lax.broadcasted_iota(I32, (HALF_CHUNK, HALF_CHUNK), 1)).astype(BF16)

        def bias_chunk(c, seen):
            for rows in halves:
                x = sc_s[c, rows, :]
                tie = (x == thr).astype(BF16)
                before = seen + _dot(earlier, tie)
                sel = ((x > thr) | ((x == thr) & (before < quota))) \
                    & ((c * KEY_CHUNK + rows.start + row) <= qpos)
                bias_s[c, rows, :] = jnp.where(sel, 0.0, NEG_INF).astype(F32)
                seen = seen + jnp.sum(tie.astype(F32), axis=0, keepdims=True)
            return seen

        lax.fori_loop(0, n_chunks, bias_chunk, jnp.zeros((1, Q_TILE), F32))

    acc_s[...] = jnp.zeros(acc_s.shape, F32)
    lane_groups = [slice(g * LANES, (g + 1) * LANES) for g in range(Q_TILE // LANES)]
    n_state = N_HEADS * len(lane_groups)

    def attn_chunk(c, state):
        ms, ls = state
        new_ms, new_ls = [], []
        for h in range(N_HEADS):
            sl = slice(h * HEAD_DIM, (h + 1) * HEAD_DIM)
            for g, qs in enumerate(lane_groups):
                m_old, l_old = ms[h * len(lane_groups) + g], ls[h * len(lane_groups) + g]
                s = _dot_nt(k_ref[c, :, sl], q_ref[qs, sl]) + bias_s[c, :, qs]
                m_new = jnp.maximum(m_old, jnp.max(_fold_rows(s, jnp.max), axis=0, keepdims=True))
                alpha = jnp.exp(m_old - m_new)
                p = jnp.exp(s - m_new)
                new_ms.append(m_new)
                new_ls.append(alpha * l_old + _fold_rows(p, jnp.sum))
                acc_s[h, :, qs] = alpha * acc_s[h, :, qs] + _dot(vt_ref[c, sl, :], p.astype(BF16))
        return tuple(new_ms), tuple(new_ls)

    _, ls = lax.fori_loop(
        0, n_chunks, attn_chunk,
        (tuple(jnp.full((1, LANES), M_INIT, F32) for _ in range(n_state)),
         tuple(jnp.zeros((SUBLANES, LANES), F32) for _ in range(n_state))))

    for h in range(N_HEADS):
        l = jnp.concatenate([jnp.sum(ls[h * len(lane_groups) + g], axis=0, keepdims=True)
                             for g in range(len(lane_groups))], axis=1)
        a_ref[:, h * HEAD_DIM:(h + 1) * HEAD_DIM] = (acc_s[h] / l).T


def _prompt_attention(q, qi_hm, wit, ki_b, k_b, vt3, *, batch, seq):
    nq = seq // Q_TILE
    nch = seq // KEY_CHUNK
    topk = min(TOPK_MAX, seq // 4)
    ki3 = ki_b.reshape(batch * nch, KEY_CHUNK, IDX_DIM)
    k3 = k_b.reshape(batch * nch, KEY_CHUNK, ATTN_WIDTH)
    resident = lambda r, n: pl.BlockSpec((nch, r, n), lambda b, i: (b, 0, 0),
                                         pipeline_mode=pl.Buffered(1))
    return pl.pallas_call(
        functools.partial(_prompt_attn_kernel, topk),
        grid=(batch, nq),
        in_specs=[pl.BlockSpec((Q_TILE, ATTN_WIDTH), lambda b, i: (b * nq + i, 0)),
                  pl.BlockSpec((IDX_HEADS, Q_TILE, IDX_DIM), lambda b, i: (0, b * nq + i, 0)),
                  pl.BlockSpec((IDX_HEADS, Q_TILE), lambda b, i: (0, b * nq + i)),
                  resident(KEY_CHUNK, IDX_DIM), resident(KEY_CHUNK, ATTN_WIDTH),
                  resident(ATTN_WIDTH, KEY_CHUNK)],
        out_specs=pl.BlockSpec((Q_TILE, ATTN_WIDTH), lambda b, i: (b * nq + i, 0)),
        out_shape=jax.ShapeDtypeStruct((batch * seq, ATTN_WIDTH), F32),
        scratch_shapes=[pltpu.VMEM((nch, KEY_CHUNK, Q_TILE), F32),
                        pltpu.VMEM((nch, KEY_CHUNK, Q_TILE), F32),
                        pltpu.VMEM((N_HEADS, HEAD_DIM, Q_TILE), F32)],
        compiler_params=pltpu.CompilerParams(dimension_semantics=("arbitrary", "arbitrary"),
                                             vmem_limit_bytes=VMEM_LIMIT),
        name="prompt_attn",
    )(q, qi_hm, wit, ki3, k3, vt3)


HALO = 16


def _gate_out_norm(a, ga, p, gb, x, wo_ref, g_ref, b_ref):
    mixed = jnp.concatenate([a * (ga * _sigmoid(ga)), p * (gb * _sigmoid(gb))], axis=-1)
    h = DEEPNORM_ALPHA * x + _dot(mixed.astype(BF16), wo_ref[...])
    mu = jnp.mean(h, axis=-1, keepdims=True)
    d = h - mu
    var = jnp.mean(d * d, axis=-1, keepdims=True)
    return d * lax.rsqrt(var + LN_EPS) * g_ref[...] + b_ref[...]


def _combine_kernel(blocks_per_seq, a_ref, ga_ref, gb_ref, u_ref, uh_ref, x_ref,
                    wp_ref, ps_ref, wo_ref, g_ref, b_ref, y_ref, ext_s):
    i = pl.program_id(0)
    tm = u_ref.shape[0]
    first = (i % blocks_per_seq) == 0
    ext_s[0:HALO, :] = jnp.where(first, 0.0, uh_ref[...])
    ext_s[HALO:HALO + tm, :] = u_ref[...]
    pos = (i % blocks_per_seq) * tm + lax.broadcasted_iota(I32, (tm, 1), 0)

    parts = []
    for g, w in enumerate(POOL_WINDOWS):
        cols = slice(g * POOL_GROUP_DIM, (g + 1) * POOL_GROUP_DIM)
        win = ext_s[HALO:HALO + tm, cols]
        for j in range(1, w):
            win = win + ext_s[HALO - j:HALO - j + tm, cols]
        count = jnp.minimum(pos + 1, w).astype(F32)
        pooled = win / count - u_ref[:, cols]
        parts.append(_dot(pooled.astype(BF16), wp_ref[g]))
    p = jnp.concatenate(parts, axis=-1) * ps_ref[...]
    y_ref[...] = _gate_out_norm(a_ref[...], ga_ref[...], p, gb_ref[...], x_ref[...],
                                wo_ref, g_ref, b_ref)


def _combine(a, ga, gb, u, x2d, w_pool_b, pool_scale, w_out_b, ln_g, ln_b, *, seq, tm):
    rows = a.shape[0]
    nb = rows // tm
    bps = seq // tm
    hpb = tm // HALO
    row_blk = lambda n: pl.BlockSpec((tm, n), lambda i: (i, 0))
    const = lambda shape: pl.BlockSpec(shape, lambda i: (0,) * len(shape))
    return pl.pallas_call(
        functools.partial(_combine_kernel, bps),
        grid=(nb,),
        in_specs=[row_blk(ATTN_WIDTH), row_blk(ATTN_WIDTH), row_blk(POOL_WIDTH), row_blk(POOL_WIDTH),
                  pl.BlockSpec((HALO, POOL_WIDTH), lambda i: (jnp.maximum(i * hpb - 1, 0), 0)),
                  row_blk(D_MODEL),
                  const((N_POOL_GROUPS, POOL_GROUP_DIM, POOL_GROUP_DIM)), const((1, POOL_WIDTH)),
                  const((D_MODEL, D_MODEL)), const((1, D_MODEL)), const((1, D_MODEL))],
        out_specs=row_blk(D_MODEL),
        out_shape=jax.ShapeDtypeStruct((rows, D_MODEL), F32),
        scratch_shapes=[pltpu.VMEM((HALO + tm, POOL_WIDTH), F32)],
        compiler_params=pltpu.CompilerParams(dimension_semantics=("arbitrary",),
                                             vmem_limit_bytes=VMEM_LIMIT),
        name="combine",
    )(a, ga, gb, u, u, x2d, w_pool_b, pool_scale, w_out_b, ln_g, ln_b)


def _sample_scores_kernel(n_pages, t_new, pt_ref, qi_ref, w_ref, kn_ref, *refs):
    page_refs, sc_ref = refs[:n_pages], refs[n_pages]
    qi = qi_ref[0]
    w = w_ref[0]

    def head_sum(s):
        r = jnp.maximum(s, 0.0) * w
        return jnp.sum(r.reshape(t_new, IDX_HEADS, s.shape[-1]), axis=1)

    for p in range(n_pages):
        kp_t = page_refs[p][0].astype(BF16)
        sc_ref[0, :, p * PAGE_SIZE:(p + 1) * PAGE_SIZE] = head_sum(_dot(qi, kp_t))
    s_new = head_sum(_dot_nt(qi, kn_ref[0]))
    t_idx = lax.broadcasted_iota(I32, s_new.shape, 0)
    j_idx = lax.broadcasted_iota(I32, s_new.shape, 1)
    sc_ref[0, :, n_pages * PAGE_SIZE:] = jnp.where(j_idx <= t_idx, s_new, NEG_INF)


def _sample_scores(page_table, qi_rows, w_col, ki_new_pad, cache_kidx_t, *, t_new):
    nb, n_pages = page_table.shape
    width = (n_pages + 1) * PAGE_SIZE
    page_spec = lambda p: pl.BlockSpec((1, IDX_DIM, PAGE_SIZE), lambda b, pt, p=p: (pt[b, p], 0, 0))
    grid_spec = pltpu.PrefetchScalarGridSpec(
        num_scalar_prefetch=1,
        grid=(nb,),
        in_specs=[pl.BlockSpec((1, t_new * IDX_HEADS, IDX_DIM), lambda b, pt: (b, 0, 0)),
                  pl.BlockSpec((1, t_new * IDX_HEADS, 1), lambda b, pt: (b, 0, 0)),
                  pl.BlockSpec((1, PAGE_SIZE, IDX_DIM), lambda b, pt: (b, 0, 0))]
                 + [page_spec(p) for p in range(n_pages)],
        out_specs=pl.BlockSpec((1, t_new, width), lambda b, pt: (b, 0, 0)),
    )
    return pl.pallas_call(
        functools.partial(_sample_scores_kernel, n_pages, t_new),
        grid_spec=grid_spec,
        out_shape=jax.ShapeDtypeStruct((nb, t_new, width), F32),
        compiler_params=pltpu.CompilerParams(dimension_semantics=("arbitrary",),
                                             vmem_limit_bytes=VMEM_LIMIT),
        name="sample_scores",
    )(page_table, qi_rows, w_col, ki_new_pad, *([cache_kidx_t] * n_pages))


def _sample_select_kernel(topk, sc_ref, sel_ref):
    rows, width = sc_ref.shape

    def count(pred):
        return jnp.sum(pred(sc_ref[...]).astype(F32), axis=1, keepdims=True)

    thr, n_ge_thr = _kth_largest(lambda cand: count(lambda x: x >= cand), topk, (rows, 1))

    surplus = (n_ge_thr > topk) & (thr > NEG_INF)
    has_ties = jnp.max(surplus.astype(F32)) > 0.5

    @pl.when(jnp.logical_not(has_ties))
    def _():
        sc = sc_ref[...]
        sel_ref[...] = jnp.where((sc >= thr) & (sc > NEG_INF), 1.0, 0.0).astype(F32)

    @pl.when(has_ties)
    def _():
        quota = topk - count(lambda x: x > thr)
        earlier = (lax.broadcasted_iota(I32, (LANES, LANES), 0)
                   < lax.broadcasted_iota(I32, (LANES, LANES), 1)).astype(BF16)
        seen = jnp.zeros((rows, 1), F32)
        for j in range(width // LANES):
            cols = slice(j * LANES, (j + 1) * LANES)
            x = sc_ref[:, cols]
            tie = (x == thr).astype(BF16)
            before = seen + _dot(tie, earlier)
            sel = ((x > thr) | ((x == thr) & (before < quota))) & (x > NEG_INF)
            sel_ref[:, cols] = jnp.where(sel, 1.0, 0.0).astype(F32)
            seen = seen + jnp.sum(tie.astype(F32), axis=1, keepdims=True)


def _sample_select(scores2d, *, topk):
    return pl.pallas_call(
        functools.partial(_sample_select_kernel, topk),
        out_shape=jax.ShapeDtypeStruct(scores2d.shape, F32),
        compiler_params=pltpu.CompilerParams(vmem_limit_bytes=VMEM_LIMIT),
        name="sample_select",
    )(scores2d)


SC_CORES = 2
SC_SUBCORES = 16
SC_LANES = 16
GATHER_ROWS = 128


def _sample_gather(sel2d, page_table, cache_k_rows, cache_v_rows, *, n_sel):
    nq = sel2d.shape[0]
    nb, n_pages = page_table.shape
    past = n_pages * PAGE_SIZE
    t_new = nq // nb
    rows_per_q = n_sel * N_HEADS
    out = jax.ShapeDtypeStruct((nq * rows_per_q, HEAD_DIM), F32)
    assert nb <= SC_CORES * SC_SUBCORES
    mesh = plsc.VectorSubcoreMesh(core_axis_name="c", subcore_axis_name="s",
                                  num_cores=SC_CORES, num_subcores=SC_SUBCORES)

    def body(sel_hbm, pt_hbm, ck_hbm, cv_hbm, kg_hbm, vg_hbm, sel_v, pt_v, key_v, row_v, buf):
        seq_id = lax.axis_index("c") * SC_SUBCORES + lax.axis_index("s")
        lane = lax.iota(I32, SC_LANES)

        @pl.when(seq_id < nb)
        def _():
            pltpu.sync_copy(pt_hbm.at[seq_id], pt_v)
            gather_sequence(seq_id, lane, sel_hbm, pt_v, ck_hbm, cv_hbm, kg_hbm, vg_hbm,
                            sel_v, key_v, row_v, buf)

    def gather_sequence(seq_id, lane, sel_hbm, pt_v, ck_hbm, cv_hbm, kg_hbm, vg_hbm,
                        sel_v, key_v, row_v, buf):
        for t in range(t_new):
            q = seq_id * t_new + t
            pltpu.sync_copy(sel_hbm.at[q, pl.ds(0, past)], sel_v)
            for j in range(key_v.shape[0] // SC_LANES):
                key_v[pl.ds(j * SC_LANES, SC_LANES)] = jnp.zeros((SC_LANES,), I32)

            def compact(j, cnt):
                m = sel_v[pl.ds(j * SC_LANES, SC_LANES)] > 0.5
                key = j * SC_LANES + lane
                page = plsc.load_gather(pt_v, [lax.shift_right_logical(key, 7)])
                phys = page * PAGE_SIZE + (key & (PAGE_SIZE - 1))
                plsc.store_compressed(key_v.at[pl.ds(cnt, SC_LANES)], phys, mask=m)
                return cnt + jnp.sum(m.astype(I32))

            lax.fori_loop(0, past // SC_LANES, compact, jnp.int32(0))

            def expand(i, carry):
                r = i * SC_LANES + lane
                k = plsc.load_gather(key_v, [lax.shift_right_logical(r, 3)])
                row_v[pl.ds(i * SC_LANES, SC_LANES)] = k * N_HEADS + (r & (N_HEADS - 1))
                return carry

            lax.fori_loop(0, rows_per_q // SC_LANES, expand, 0)

            for w in range(rows_per_q // GATHER_ROWS):
                idx = row_v.at[pl.ds(w * GATHER_ROWS, GATHER_ROWS)]
                dst = pl.ds(q * rows_per_q + w * GATHER_ROWS, GATHER_ROWS)
                pltpu.sync_copy(ck_hbm.at[idx], buf)
                pltpu.sync_copy(buf, kg_hbm.at[dst])
                pltpu.sync_copy(cv_hbm.at[idx], buf)
                pltpu.sync_copy(buf, vg_hbm.at[dst])

    return pl.kernel(
        body, out_type=(out, out), mesh=mesh,
        scratch_types=[pltpu.VMEM((past,), F32), pltpu.VMEM((n_pages,), I32),
                       pltpu.VMEM((n_sel + SC_LANES,), I32), pltpu.VMEM((rows_per_q,), I32),
                       pltpu.VMEM((GATHER_ROWS, HEAD_DIM), F32)],
        compiler_params=pltpu.CompilerParams(needs_layout_passes=False),
        name="sample_gather",
    )(sel2d, page_table, cache_k_rows, cache_v_rows)


def _gathered_attn_kernel(t_new, n_sel, past, sel_ref, q_ref, kg_ref, vg_ref, kn_ref, vn_ref, a_ref):
    cols = n_sel * N_HEADS
    row = lax.broadcasted_iota(I32, (N_HEADS, cols), 0)
    col = lax.broadcasted_iota(I32, (N_HEADS, cols), 1)
    own_head = (col & (N_HEADS - 1)) == row
    slot = lax.shift_right_logical(col, 3)
    ncol = t_new * N_HEADS
    rown = lax.broadcasted_iota(I32, (N_HEADS, ncol), 0)
    coln = lax.broadcasted_iota(I32, (N_HEADS, ncol), 1)
    own_head_new = (coln & (N_HEADS - 1)) == rown
    new_key = lax.shift_right_logical(coln, 3)

    for t in range(t_new):
        heads = slice(t * N_HEADS, (t + 1) * N_HEADS)
        q = q_ref[0, heads, :]
        sel_past = sel_ref[0, t:t + 1, :past]
        n_past = jnp.sum(sel_past, axis=1, keepdims=True).astype(I32)
        s = jnp.where(own_head & (slot < n_past), _dot_nt(q, kg_ref[t].astype(BF16)), NEG_INF)

        taken = jnp.zeros((N_HEADS, ncol), jnp.bool_)
        for j in range(t_new):
            taken = taken | ((new_key == j) & (sel_ref[0, t:t + 1, past + j:past + j + 1] > 0.5))
        sn = jnp.where(taken & own_head_new, _dot_nt(q, kn_ref[0]), NEG_INF)

        m = jnp.maximum(jnp.max(s, axis=1, keepdims=True), jnp.max(sn, axis=1, keepdims=True))
        p, pn = jnp.exp(s - m), jnp.exp(sn - m)
        l = jnp.sum(p, axis=1, keepdims=True) + jnp.sum(pn, axis=1, keepdims=True)
        o = _dot(p.astype(BF16), vg_ref[t].astype(BF16)) + _dot(pn.astype(BF16), vn_ref[0])
        a_ref[0, heads, :] = o / l


def _gathered_attention(sel3, q_rows, kg, vg, k_new, v_new, *, t_new, n_sel, past):
    nb, _, width = sel3.shape
    cols = n_sel * N_HEADS
    rows = t_new * N_HEADS
    per_seq = lambda r, n: pl.BlockSpec((1, r, n), lambda b: (b, 0, 0))
    gathered = pl.BlockSpec((t_new, cols, HEAD_DIM), lambda b: (b, 0, 0))
    return pl.pallas_call(
        functools.partial(_gathered_attn_kernel, t_new, n_sel, past),
        grid=(nb,),
        in_specs=[per_seq(t_new, width), per_seq(rows, HEAD_DIM), gathered, gathered,
                  per_seq(rows, HEAD_DIM), per_seq(rows, HEAD_DIM)],
        out_specs=per_seq(rows, HEAD_DIM),
        out_shape=jax.ShapeDtypeStruct((nb, rows, HEAD_DIM), F32),
        compiler_params=pltpu.CompilerParams(dimension_semantics=("arbitrary",),
                                             vmem_limit_bytes=VMEM_LIMIT),
        name="gathered_attn",
    )(sel3, q_rows, kg.reshape(nb * t_new, cols, HEAD_DIM), vg.reshape(nb * t_new, cols, HEAD_DIM),
      k_new, v_new)


def _sample_combine_kernel(t_new, a_ref, ga_ref, gb_ref, ext_ref, x_ref,
                           wp_ref, ps_ref, wo_ref, g_ref, b_ref, y_ref):
    p_t = []
    for t in range(t_new):
        parts = []
        for g, w in enumerate(POOL_WINDOWS):
            cols = slice(g * POOL_GROUP_DIM, (g + 1) * POOL_GROUP_DIM)
            cur = POOL_STATE + t
            win = ext_ref[cur, :, cols]
            for j in range(1, w):
                win = win + ext_ref[cur - j, :, cols]
            pooled = win / float(w) - ext_ref[cur, :, cols]
            parts.append(_dot(pooled.astype(BF16), wp_ref[g]))
        p_t.append(jnp.concatenate(parts, axis=-1) * ps_ref[...])
    y_ref[...] = _gate_out_norm(a_ref[...], ga_ref[...], jnp.concatenate(p_t, axis=0), gb_ref[...],
                                x_ref[...], wo_ref, g_ref, b_ref)


def _sample_combine(a, ga, gb, ext_tm, x2d, w_pool_b, pool_scale, w_out_b, ln_g, ln_b, *, t_new):
    return pl.pallas_call(
        functools.partial(_sample_combine_kernel, t_new),
        out_shape=jax.ShapeDtypeStruct(x2d.shape, F32),
        compiler_params=pltpu.CompilerParams(vmem_limit_bytes=VMEM_LIMIT),
        name="sample_combine",
    )(a, ga, gb, ext_tm, x2d, w_pool_b, pool_scale, w_out_b, ln_g, ln_b)


def kernel(x_prompt, x_sample, cache_k, cache_v, cache_kidx, state_pool, page_table,
           w_in, w_pool, pool_scale, w_out, ln_g, ln_b):
    assert w_in.shape[0] == DEPTH == 1
    batch, seq, _ = x_prompt.shape
    nb, t_new, _ = x_sample.shape
    n_pool, n_pages = cache_k.shape[1], page_table.shape[1]
    past = n_pages * PAGE_SIZE

    w_t = jnp.swapaxes(w_in[0], 0, 1).astype(BF16)
    w_pool_b = w_pool[0].astype(BF16)
    w_out_b = w_out[0].astype(BF16)
    ps, g, b = pool_scale[0][None, :], ln_g[0][None, :], ln_b[0][None, :]

    xp2d = x_prompt.reshape(batch * seq, D_MODEL)
    (q, kf, kb, vf, vb, ga, qi_hm, _, kib, kit, wit, u, gb) = _project(
        xp2d, w_t, np.arange(seq), tm=KEY_CHUNK, rows_per_seq=seq, head_major=True)
    a = _prompt_attention(q, qi_hm, wit, kib, kb, vb, batch=batch, seq=seq)
    y_prompt = _combine(a, ga, gb, u, xp2d, w_pool_b, ps, w_out_b, g, b, seq=seq, tm=256)

    rows_s = nb * t_new
    xs2d = x_sample.reshape(rows_s, D_MODEL)
    (qs, kfs, kbs, vfs, vbs, gas, qis, kifs, kibs, _, wits, us, gbs) = _project(
        xs2d, w_t, np.tile(past + np.arange(t_new), nb), tm=rows_s, rows_per_seq=rows_s,
        head_major=False)

    per_seq = lambda z, r, n: z.reshape(nb, r, n)
    pad_rows = lambda z, n: jnp.pad(z, ((0, 0), (0, n - z.shape[1]), (0, 0)))
    scores = _sample_scores(
        page_table,
        per_seq(qis, t_new * IDX_HEADS, IDX_DIM),
        per_seq(wits.T, t_new * IDX_HEADS, 1),
        pad_rows(per_seq(kibs, t_new, IDX_DIM), PAGE_SIZE),
        jnp.swapaxes(cache_kidx[0], 1, 2),
        t_new=t_new)
    width = scores.shape[-1]
    sel = _sample_select(scores.reshape(rows_s, width), topk=min(TOPK_MAX, (past + t_new) // 4))
    head_rows = t_new * N_HEADS
    n_sel = min(TOPK_MAX, (past + t_new) // 4)
    kg, vg = _sample_gather(sel, page_table,
                            cache_k.reshape(n_pool * PAGE_SIZE * N_HEADS, HEAD_DIM),
                            cache_v.reshape(n_pool * PAGE_SIZE * N_HEADS, HEAD_DIM), n_sel=n_sel)
    a_s = _gathered_attention(
        per_seq(sel, t_new, width), per_seq(qs, head_rows, HEAD_DIM), kg, vg,
        per_seq(kbs, head_rows, HEAD_DIM), per_seq(vbs, head_rows, HEAD_DIM),
        t_new=t_new, n_sel=n_sel, past=past).reshape(rows_s, ATTN_WIDTH)

    u_ext = jnp.concatenate([state_pool[0], us.reshape(nb, t_new, POOL_WIDTH)], axis=1)
    time_major = lambda z: z.reshape(nb, t_new, z.shape[-1]).transpose(1, 0, 2).reshape(rows_s, z.shape[-1])
    y_tm = _sample_combine(time_major(a_s), time_major(gas), time_major(gbs), u_ext.transpose(1, 0, 2),
                           time_major(xs2d), w_pool_b, ps, w_out_b, g, b, t_new=t_new)
    y_sample = y_tm.reshape(t_new, nb, D_MODEL).transpose(1, 0, 2)

    hd = (N_HEADS, HEAD_DIM)
    return (y_prompt.reshape(batch, seq, D_MODEL),
            y_sample,
            kf.reshape(1, batch, seq, *hd), vf.reshape(1, batch, seq, *hd),
            jnp.swapaxes(kit, 1, 2)[None],
            u.reshape(batch, seq, POOL_WIDTH)[None, :, -POOL_STATE:],
            kfs.reshape(1, nb, t_new, *hd), vfs.reshape(1, nb, t_new, *hd),
            kifs.reshape(1, nb, t_new, IDX_DIM),
            u_ext[None, :, -POOL_STATE:])
```

```python
import functools

import numpy as np
import jax
import jax.numpy as jnp
from jax import lax
from jax.experimental import pallas as pl
from jax.experimental.pallas import tpu as pltpu
from jax.experimental.pallas import tpu_sc as plsc

F32 = jnp.float32
BF16 = jnp.bfloat16
I32 = jnp.int32

D_MODEL = 2048
PAGE_SIZE = 128
N_HEADS = 8
HEAD_DIM = 128
ATTN_WIDTH = N_HEADS * HEAD_DIM
ROT_DIM = HEAD_DIM // 4
ROPE_THETA = 500000.0
IDX_HEADS = 16
IDX_DIM = 64
IDX_ROT_DIM = IDX_DIM // 4
TOPK_MAX = 256
POOL_WINDOWS = (2, 4, 8, 16)
N_POOL_GROUPS = len(POOL_WINDOWS)
POOL_WIDTH = D_MODEL - ATTN_WIDTH
POOL_GROUP_DIM = POOL_WIDTH // N_POOL_GROUPS
POOL_STATE = max(POOL_WINDOWS) - 1
IN_SPLITS = (ATTN_WIDTH, ATTN_WIDTH, ATTN_WIDTH, ATTN_WIDTH,
             IDX_HEADS * IDX_DIM, IDX_DIM, IDX_HEADS, POOL_WIDTH, POOL_WIDTH)
IN_COLS = sum(IN_SPLITS)
DEPTH = 1
DEEPNORM_ALPHA = (2 * DEPTH) ** 0.25
LN_EPS = 1e-5
ATTN_SCALE = HEAD_DIM ** -0.5
LOG2_E = 1.4426950408889634
IDX_SCALE = (IDX_DIM ** -0.5) * (IDX_HEADS ** -0.5)

LANES = 128
SUBLANES = 8
VMEM_LIMIT = 56 * 1024 * 1024

(_OFF_Q, _OFF_K, _OFF_V, _OFF_GA, _OFF_QI, _OFF_KI, _OFF_WI, _OFF_U, _OFF_GB) = (
    [0] + np.cumsum(IN_SPLITS)[:-1].tolist())

NEG_INF = float("-inf")
M_INIT = -1e30
INT_MIN = -2 ** 31
KEY_OF_NEG_INF = -2139095041

KEY_CHUNK = 256
HALF_CHUNK = 128
Q_TILE = 256


def _dot(a, b):
    return jnp.dot(a, b, preferred_element_type=F32)


def _dot_nt(a, b):
    return lax.dot_general(a, b, (((1,), (1,)), ((), ())), preferred_element_type=F32)


def _sigmoid(x):
    return 1.0 / (1.0 + jnp.exp(-x))


def _rope_angles(pos, rot):
    half = rot // 2
    freqs = ROPE_THETA ** (-np.arange(half, dtype=np.float64) * (2.0 / rot))
    ang = np.asarray(pos, np.float64)[:, None] * freqs[None, :]
    return np.cos(ang), np.sin(ang)


def _rope_tables_t(pos, rot):
    cos, sin = _rope_angles(pos, rot)
    return jnp.asarray(np.stack([cos.T, sin.T]), F32)


def _rope_tables(pos, rot, group):
    half = rot // 2
    cos, sin = _rope_angles(pos, rot)
    lane = np.arange(LANES) % group
    sel = lane % half
    cos_l, sin_l = cos[:, sel], sin[:, sel]
    first = (lane < half)[None, :]
    second = ((lane >= half) & (lane < rot))[None, :]
    c = np.where(first | second, cos_l, 1.0)
    s1 = np.where(first, -sin_l, 0.0)
    s2 = np.where(second, sin_l, 0.0)
    return jnp.asarray(np.stack([c, s1, s2]), F32)


def _rope(x, tab_ref, half):
    fwd = pltpu.roll(x, LANES - half, 1)
    bwd = pltpu.roll(x, half, 1)
    return x * tab_ref[0] + fwd * tab_ref[1] + bwd * tab_ref[2]


def _proj_kernel(head_major, x_ref, wt_ref, th_ref, ti_ref, tit_ref,
                 q_ref, kf_ref, kb_ref, vf_ref, vb_ref, ga_ref, qi_ref, kif_ref, kib_ref, kit_ref,
                 wit_ref, u_ref, gb_ref):
    xb = x_ref[...].astype(BF16)

    def seg(off, n):
        return _dot_nt(xb, wt_ref[off:off + n, :])

    def seg_t(off, n):
        return _dot_nt(wt_ref[off:off + n, :], xb)

    qf, kf = seg(_OFF_Q, ATTN_WIDTH), seg(_OFF_K, ATTN_WIDTH)
    for h in range(N_HEADS):
        sl = slice(h * HEAD_DIM, (h + 1) * HEAD_DIM)
        qh = _rope(qf[:, sl], th_ref, ROT_DIM // 2)
        q_ref[:, sl] = (qh * (ATTN_SCALE * LOG2_E if head_major else ATTN_SCALE)).astype(BF16)
        kh = _rope(kf[:, sl], th_ref, ROT_DIM // 2)
        kf_ref[:, sl] = kh
        kb_ref[:, sl] = kh.astype(BF16)

    v = seg(_OFF_V, ATTN_WIDTH)
    vf_ref[...] = v
    if head_major:
        vb_ref[0] = v.T.astype(BF16)
    else:
        vb_ref[...] = v.astype(BF16)
    ga_ref[...] = seg(_OFF_GA, ATTN_WIDTH)
    u_ref[...] = seg(_OFF_U, POOL_WIDTH)
    gb_ref[...] = seg(_OFF_GB, POOL_WIDTH)

    qif = seg(_OFF_QI, IDX_HEADS * IDX_DIM)
    for j in range(IDX_HEADS // 2):
        r = _rope(qif[:, j * LANES:(j + 1) * LANES], ti_ref, IDX_ROT_DIM // 2).astype(BF16)
        if head_major:
            qi_ref[2 * j] = r[:, :IDX_DIM]
            qi_ref[2 * j + 1] = r[:, IDX_DIM:]
        else:
            qi_ref[:, j * LANES:(j + 1) * LANES] = r

    half = IDX_ROT_DIM // 2
    kt = seg_t(_OFF_KI, IDX_DIM)
    x1, x2 = kt[:half], kt[half:2 * half]
    cos_t, sin_t = tit_ref[0], tit_ref[1]
    kit_ref[0] = jnp.concatenate([x1 * cos_t - x2 * sin_t, x1 * sin_t + x2 * cos_t, kt[2 * half:]], axis=0)
    kn = jnp.concatenate([seg(_OFF_KI, IDX_DIM), jnp.zeros((xb.shape[0], LANES - IDX_DIM), F32)], axis=1)
    ki = _rope(kn, ti_ref, half)[:, :IDX_DIM]
    kif_ref[...] = ki
    kib_ref[...] = ki.astype(BF16)

    wit_ref[...] = seg_t(_OFF_WI, IDX_HEADS) * IDX_SCALE


def _project(x2d, w_t, pos, *, tm, rows_per_seq, head_major):
    rows = x2d.shape[0]
    nb = rows // tm
    pos_blocks = pos.shape[0] // tm
    seq_blocks = rows_per_seq // tm
    tab_h = _rope_tables(pos, ROT_DIM, HEAD_DIM)
    tab_i = _rope_tables(pos, IDX_ROT_DIM, IDX_DIM)
    tab_it = _rope_tables_t(pos, IDX_ROT_DIM)

    row_blk = lambda n: pl.BlockSpec((tm, n), lambda i: (i, 0))
    tab_blk = pl.BlockSpec((3, tm, LANES), lambda i: (0, i % pos_blocks, 0))
    tabt_blk = pl.BlockSpec((2, IDX_ROT_DIM // 2, tm), lambda i: (0, 0, i % pos_blocks))
    wide = lambda dt: jax.ShapeDtypeStruct((rows, ATTN_WIDTH), dt)
    if head_major:
        qi_shape = jax.ShapeDtypeStruct((IDX_HEADS, rows, IDX_DIM), BF16)
        qi_blk = pl.BlockSpec((IDX_HEADS, tm, IDX_DIM), lambda i: (0, i, 0))
        vb_shape = jax.ShapeDtypeStruct((nb, ATTN_WIDTH, tm), BF16)
        vb_blk = pl.BlockSpec((1, ATTN_WIDTH, tm), lambda i: (i, 0, 0))
    else:
        qi_shape = jax.ShapeDtypeStruct((rows, IDX_HEADS * IDX_DIM), BF16)
        qi_blk = row_blk(IDX_HEADS * IDX_DIM)
        vb_shape, vb_blk = wide(BF16), row_blk(ATTN_WIDTH)

    out_shape = (wide(BF16), wide(F32), wide(BF16), wide(F32), vb_shape, wide(F32),
                 qi_shape,
                 jax.ShapeDtypeStruct((rows, IDX_DIM), F32), jax.ShapeDtypeStruct((rows, IDX_DIM), BF16),
                 jax.ShapeDtypeStruct((rows // rows_per_seq, IDX_DIM, rows_per_seq), F32),
                 jax.ShapeDtypeStruct((IDX_HEADS, rows), F32), wide(F32), wide(F32))
    out_specs = (row_blk(ATTN_WIDTH),) * 4 + (vb_blk, row_blk(ATTN_WIDTH)) + (
        qi_blk, row_blk(IDX_DIM), row_blk(IDX_DIM),
        pl.BlockSpec((1, IDX_DIM, tm), lambda i: (i // seq_blocks, 0, i % seq_blocks)),
        pl.BlockSpec((IDX_HEADS, tm), lambda i: (0, i)),
        row_blk(POOL_WIDTH), row_blk(POOL_WIDTH))
    return pl.pallas_call(
        functools.partial(_proj_kernel, head_major),
        grid=(nb,),
        in_specs=[row_blk(D_MODEL),
                  pl.BlockSpec((IN_COLS, D_MODEL), lambda i: (0, 0), pipeline_mode=pl.Buffered(1)),
                  tab_blk, tab_blk, tabt_blk],
        out_specs=out_specs,
        out_shape=out_shape,
        compiler_params=pltpu.CompilerParams(dimension_semantics=("arbitrary",),
                                             vmem_limit_bytes=VMEM_LIMIT),
        name="proj_hm" if head_major else "proj_rm",
    )(x2d, w_t, tab_h, tab_i, tab_it)


def _key_to_float(key):
    return pltpu.bitcast(key ^ ((key >> 31) & 0x7FFFFFFF), F32)


def _kth_largest(count_ge, topk, shape):
    def bit_body(it, carry):
        prefix, cnt_at = carry
        trial = prefix | jnp.left_shift(jnp.int32(1), 31 - it)
        cnt = count_ge(_key_to_float(trial ^ INT_MIN))
        take = cnt >= topk
        return jnp.where(take, trial, prefix), jnp.where(take, cnt, cnt_at)

    prefix, cnt_at = lax.fori_loop(0, 32, bit_body, (jnp.zeros(shape, I32), jnp.zeros(shape, F32)))
    return _key_to_float(jnp.maximum(prefix ^ INT_MIN, KEY_OF_NEG_INF)), cnt_at


def _fold_rows(x, op):
    return op(x.reshape(x.shape[0] // SUBLANES, SUBLANES, x.shape[1]), axis=0)


def _prompt_attn_kernel(topk, q_ref, qi_ref, wit_ref, ki_ref, k_ref, vt_ref, a_ref,
                        sc_s, bias_s, acc_s):
    i = pl.program_id(1)
    n_chunks = (i * Q_TILE + Q_TILE + KEY_CHUNK - 1) // KEY_CHUNK
    qpos = i * Q_TILE + lax.broadcasted_iota(I32, (1, Q_TILE), 1)
    row = lax.broadcasted_iota(I32, (HALF_CHUNK, 1), 0)
    halves = [slice(r * HALF_CHUNK, (r + 1) * HALF_CHUNK) for r in range(KEY_CHUNK // HALF_CHUNK)]

    def score_chunk(c, carry):
        for rows in halves:
            kic = ki_ref[c, rows, :]
            acc = jnp.zeros((HALF_CHUNK, Q_TILE), F32)
            for h in range(IDX_HEADS):
                st = _dot_nt(kic, qi_ref[h])
                acc = acc + jnp.maximum(st, 0.0) * wit_ref[h:h + 1, :]
            causal = (c * KEY_CHUNK + rows.start + row) <= qpos
            sc_s[c, rows, :] = jnp.where(causal, acc, NEG_INF)
        return carry

    lax.fori_loop(0, n_chunks, score_chunk, 0)

    def count(pred):
        def body(c, cnt):
            for rows in halves:
                cnt = cnt + _fold_rows(pred(sc_s[c, rows, :]).astype(F32), jnp.sum)
            return cnt
        cnt = lax.fori_loop(0, n_chunks, body, jnp.zeros((SUBLANES, Q_TILE), F32))
        return jnp.sum(cnt, axis=0, keepdims=True)

    thr, n_ge_thr = _kth_largest(lambda cand: count(lambda x: x >= cand), topk, (1, Q_TILE))

    surplus = (n_ge_thr > topk) & (thr > NEG_INF)
    has_ties = jnp.max(surplus.astype(F32)) > 0.5

    @pl.when(jnp.logical_not(has_ties))
    def _():
        def bias_chunk(c, carry):
            for rows in halves:
                sel = (sc_s[c, rows, :] >= thr) & ((c * KEY_CHUNK + rows.start + row) <= qpos)
                bias_s[c, rows, :] = jnp.where(sel, 0.0, NEG_INF).astype(F32)
            return carry

        lax.fori_loop(0, n_chunks, bias_chunk, 0)

    @pl.when(has_ties)
    def _():
        quota = topk - count(lambda x: x > thr)
        earlier = (lax.broadcasted_iota(I32, (HALF_CHUNK, HALF_CHUNK), 0)
                   > lax.broadcasted_iota(I32, (HALF_CHUNK, HALF_CHUNK), 1)).astype(BF16)

        def bias_chunk(c, seen):
            for rows in halves:
                x = sc_s[c, rows, :]
                tie = (x == thr).astype(BF16)
                before = seen + _dot(earlier, tie)
                sel = ((x > thr) | ((x == thr) & (before < quota))) \
                    & ((c * KEY_CHUNK + rows.start + row) <= qpos)
                bias_s[c, rows, :] = jnp.where(sel, 0.0, NEG_INF).astype(F32)
                seen = seen + jnp.sum(tie.astype(F32), axis=0, keepdims=True)
            return seen

        lax.fori_loop(0, n_chunks, bias_chunk, jnp.zeros((1, Q_TILE), F32))

    acc_s[...] = jnp.zeros(acc_s.shape, F32)
    lane_groups = [slice(g * LANES, (g + 1) * LANES) for g in range(Q_TILE // LANES)]
    n_state = N_HEADS * len(lane_groups)

    def attn_chunk(c, state):
        ms, ls = state
        new_ms, new_ls = [], []
        for h in range(N_HEADS):
            sl = slice(h * HEAD_DIM, (h + 1) * HEAD_DIM)
            for g, qs in enumerate(lane_groups):
                m_old, l_old = ms[h * len(lane_groups) + g], ls[h * len(lane_groups) + g]
                s = _dot_nt(k_ref[c, :, sl], q_ref[qs, sl]) + bias_s[c, :, qs]
                m_new = jnp.maximum(m_old, jnp.max(_fold_rows(s, jnp.max), axis=0, keepdims=True))
                alpha = jnp.exp2(m_old - m_new)
                p = jnp.exp2(s - m_new)
                new_ms.append(m_new)
                new_ls.append(alpha * l_old + _fold_rows(p, jnp.sum))
                acc_s[h, :, qs] = alpha * acc_s[h, :, qs] + _dot(vt_ref[c, sl, :], p.astype(BF16))
        return tuple(new_ms), tuple(new_ls)

    _, ls = lax.fori_loop(
        0, n_chunks, attn_chunk,
        (tuple(jnp.full((1, LANES), M_INIT, F32) for _ in range(n_state)),
         tuple(jnp.zeros((SUBLANES, LANES), F32) for _ in range(n_state))))

    for h in range(N_HEADS):
        l = jnp.concatenate([jnp.sum(ls[h * len(lane_groups) + g], axis=0, keepdims=True)
                             for g in range(len(lane_groups))], axis=1)
        a_ref[:, h * HEAD_DIM:(h + 1) * HEAD_DIM] = (acc_s[h] / l).T


def _prompt_attention(q, qi_hm, wit, ki_b, k_b, vt3, *, batch, seq):
    nq = seq // Q_TILE
    nch = seq // KEY_CHUNK
    topk = min(TOPK_MAX, seq // 4)
    ki3 = ki_b.reshape(batch * nch, KEY_CHUNK, IDX_DIM)
    k3 = k_b.reshape(batch * nch, KEY_CHUNK, ATTN_WIDTH)
    resident = lambda r, n: pl.BlockSpec((nch, r, n), lambda b, i: (b, 0, 0),
                                         pipeline_mode=pl.Buffered(1))
    return pl.pallas_call(
        functools.partial(_prompt_attn_kernel, topk),
        grid=(batch, nq),
        in_specs=[pl.BlockSpec((Q_TILE, ATTN_WIDTH), lambda b, i: (b * nq + i, 0)),
                  pl.BlockSpec((IDX_HEADS, Q_TILE, IDX_DIM), lambda b, i: (0, b * nq + i, 0)),
                  pl.BlockSpec((IDX_HEADS, Q_TILE), lambda b, i: (0, b * nq + i)),
                  resident(KEY_CHUNK, IDX_DIM), resident(KEY_CHUNK, ATTN_WIDTH),
                  resident(ATTN_WIDTH, KEY_CHUNK)],
        out_specs=pl.BlockSpec((Q_TILE, ATTN_WIDTH), lambda b, i: (b * nq + i, 0)),
        out_shape=jax.ShapeDtypeStruct((batch * seq, ATTN_WIDTH), F32),
        scratch_shapes=[pltpu.VMEM((nch, KEY_CHUNK, Q_TILE), F32),
                        pltpu.VMEM((nch, KEY_CHUNK, Q_TILE), F32),
                        pltpu.VMEM((N_HEADS, HEAD_DIM, Q_TILE), F32)],
        compiler_params=pltpu.CompilerParams(dimension_semantics=("arbitrary", "arbitrary"),
                                             vmem_limit_bytes=VMEM_LIMIT),
        name="prompt_attn",
    )(q, qi_hm, wit, ki3, k3, vt3)


HALO = 16


def _gate_out_norm(a, ga, p, gb, x, wo_ref, g_ref, b_ref):
    mixed = jnp.concatenate([a * (ga * _sigmoid(ga)), p * (gb * _sigmoid(gb))], axis=-1)
    h = DEEPNORM_ALPHA * x + _dot(mixed.astype(BF16), wo_ref[...])
    mu = jnp.mean(h, axis=-1, keepdims=True)
    d = h - mu
    var = jnp.mean(d * d, axis=-1, keepdims=True)
    return d * lax.rsqrt(var + LN_EPS) * g_ref[...] + b_ref[...]


def _combine_kernel(blocks_per_seq, a_ref, ga_ref, gb_ref, u_ref, uh_ref, x_ref,
                    wp_ref, ps_ref, wo_ref, g_ref, b_ref, y_ref, ext_s):
    i = pl.program_id(0)
    tm = u_ref.shape[0]
    first = (i % blocks_per_seq) == 0
    ext_s[0:HALO, :] = jnp.where(first, 0.0, uh_ref[...])
    ext_s[HALO:HALO + tm, :] = u_ref[...]
    pos = (i % blocks_per_seq) * tm + lax.broadcasted_iota(I32, (tm, 1), 0)

    parts = []
    for g, w in enumerate(POOL_WINDOWS):
        cols = slice(g * POOL_GROUP_DIM, (g + 1) * POOL_GROUP_DIM)
        win = ext_s[HALO:HALO + tm, cols]
        for j in range(1, w):
            win = win + ext_s[HALO - j:HALO - j + tm, cols]
        count = jnp.minimum(pos + 1, w).astype(F32)
        pooled = win / count - u_ref[:, cols]
        parts.append(_dot(pooled.astype(BF16), wp_ref[g]))
    p = jnp.concatenate(parts, axis=-1) * ps_ref[...]
    y_ref[...] = _gate_out_norm(a_ref[...], ga_ref[...], p, gb_ref[...], x_ref[...],
                                wo_ref, g_ref, b_ref)


def _combine(a, ga, gb, u, x2d, w_pool_b, pool_scale, w_out_b, ln_g, ln_b, *, seq, tm):
    rows = a.shape[0]
    nb = rows // tm
    bps = seq // tm
    hpb = tm // HALO
    row_blk = lambda n: pl.BlockSpec((tm, n), lambda i: (i, 0))
    const = lambda shape: pl.BlockSpec(shape, lambda i: (0,) * len(shape))
    return pl.pallas_call(
        functools.partial(_combine_kernel, bps),
        grid=(nb,),
        in_specs=[row_blk(ATTN_WIDTH), row_blk(ATTN_WIDTH), row_blk(POOL_WIDTH), row_blk(POOL_WIDTH),
                  pl.BlockSpec((HALO, POOL_WIDTH), lambda i: (jnp.maximum(i * hpb - 1, 0), 0)),
                  row_blk(D_MODEL),
                  const((N_POOL_GROUPS, POOL_GROUP_DIM, POOL_GROUP_DIM)), const((1, POOL_WIDTH)),
                  const((D_MODEL, D_MODEL)), const((1, D_MODEL)), const((1, D_MODEL))],
        out_specs=row_blk(D_MODEL),
        out_shape=jax.ShapeDtypeStruct((rows, D_MODEL), F32),
        scratch_shapes=[pltpu.VMEM((HALO + tm, POOL_WIDTH), F32)],
        compiler_params=pltpu.CompilerParams(dimension_semantics=("arbitrary",),
                                             vmem_limit_bytes=VMEM_LIMIT),
        name="combine",
    )(a, ga, gb, u, u, x2d, w_pool_b, pool_scale, w_out_b, ln_g, ln_b)


def _sample_scores_kernel(n_pages, t_new, pt_ref, qi_ref, w_ref, kn_ref, *refs):
    page_refs, sc_ref = refs[:n_pages], refs[n_pages]
    qi = qi_ref[0]
    w = w_ref[0]

    def head_sum(s):
        r = jnp.maximum(s, 0.0) * w
        return jnp.sum(r.reshape(t_new, IDX_HEADS, s.shape[-1]), axis=1)

    for p in range(n_pages):
        kp_t = page_refs[p][0].astype(BF16)
        sc_ref[0, :, p * PAGE_SIZE:(p + 1) * PAGE_SIZE] = head_sum(_dot(qi, kp_t))
    s_new = head_sum(_dot_nt(qi, kn_ref[0]))
    t_idx = lax.broadcasted_iota(I32, s_new.shape, 0)
    j_idx = lax.broadcasted_iota(I32, s_new.shape, 1)
    sc_ref[0, :, n_pages * PAGE_SIZE:] = jnp.where(j_idx <= t_idx, s_new, NEG_INF)


def _sample_scores(page_table, qi_rows, w_col, ki_new_pad, cache_kidx_t, *, t_new):
    nb, n_pages = page_table.shape
    width = (n_pages + 1) * PAGE_SIZE
    page_spec = lambda p: pl.BlockSpec((1, IDX_DIM, PAGE_SIZE), lambda b, pt, p=p: (pt[b, p], 0, 0))
    grid_spec = pltpu.PrefetchScalarGridSpec(
        num_scalar_prefetch=1,
        grid=(nb,),
        in_specs=[pl.BlockSpec((1, t_new * IDX_HEADS, IDX_DIM), lambda b, pt: (b, 0, 0)),
                  pl.BlockSpec((1, t_new * IDX_HEADS, 1), lambda b, pt: (b, 0, 0)),
                  pl.BlockSpec((1, PAGE_SIZE, IDX_DIM), lambda b, pt: (b, 0, 0))]
                 + [page_spec(p) for p in range(n_pages)],
        out_specs=pl.BlockSpec((1, t_new, width), lambda b, pt: (b, 0, 0)),
    )
    return pl.pallas_call(
        functools.partial(_sample_scores_kernel, n_pages, t_new),
        grid_spec=grid_spec,
        out_shape=jax.ShapeDtypeStruct((nb, t_new, width), F32),
        compiler_params=pltpu.CompilerParams(dimension_semantics=("arbitrary",),
                                             vmem_limit_bytes=VMEM_LIMIT),
        name="sample_scores",
    )(page_table, qi_rows, w_col, ki_new_pad, *([cache_kidx_t] * n_pages))


def _sample_select_kernel(topk, sc_ref, sel_ref):
    rows, width = sc_ref.shape

    def count(pred):
        return jnp.sum(pred(sc_ref[...]).astype(F32), axis=1, keepdims=True)

    thr, n_ge_thr = _kth_largest(lambda cand: count(lambda x: x >= cand), topk, (rows, 1))

    surplus = (n_ge_thr > topk) & (thr > NEG_INF)
    has_ties = jnp.max(surplus.astype(F32)) > 0.5

    @pl.when(jnp.logical_not(has_ties))
    def _():
        sc = sc_ref[...]
        sel_ref[...] = jnp.where((sc >= thr) & (sc > NEG_INF), 1.0, 0.0).astype(F32)

    @pl.when(has_ties)
    def _():
        quota = topk - count(lambda x: x > thr)
        earlier = (lax.broadcasted_iota(I32, (LANES, LANES), 0)
                   < lax.broadcasted_iota(I32, (LANES, LANES), 1)).astype(BF16)
        seen = jnp.zeros((rows, 1), F32)
        for j in range(width // LANES):
            cols = slice(j * LANES, (j + 1) * LANES)
            x = sc_ref[:, cols]
            tie = (x == thr).astype(BF16)
            before = seen + _dot(tie, earlier)
            sel = ((x > thr) | ((x == thr) & (before < quota))) & (x > NEG_INF)
            sel_ref[:, cols] = jnp.where(sel, 1.0, 0.0).astype(F32)
            seen = seen + jnp.sum(tie.astype(F32), axis=1, keepdims=True)


def _sample_select(scores2d, *, topk):
    return pl.pallas_call(
        functools.partial(_sample_select_kernel, topk),
        out_shape=jax.ShapeDtypeStruct(scores2d.shape, F32),
        compiler_params=pltpu.CompilerParams(vmem_limit_bytes=VMEM_LIMIT),
        name="sample_select",
    )(scores2d)


SC_CORES = 2
SC_SUBCORES = 16
SC_LANES = 16
GATHER_ROWS = 128


def _sample_gather(sel2d, page_table, cache_k_rows, cache_v_rows, *, n_sel):
    nq = sel2d.shape[0]
    nb, n_pages = page_table.shape
    past = n_pages * PAGE_SIZE
    t_new = nq // nb
    rows_per_q = n_sel * N_HEADS
    out = jax.ShapeDtypeStruct((nq * rows_per_q, HEAD_DIM), F32)
    assert nb <= SC_CORES * SC_SUBCORES
    mesh = plsc.VectorSubcoreMesh(core_axis_name="c", subcore_axis_name="s",
                                  num_cores=SC_CORES, num_subcores=SC_SUBCORES)

    def body(sel_hbm, pt_hbm, ck_hbm, cv_hbm, kg_hbm, vg_hbm, sel_v, pt_v, key_v, row_v, buf):
        seq_id = lax.axis_index("c") * SC_SUBCORES + lax.axis_index("s")
        lane = lax.iota(I32, SC_LANES)

        @pl.when(seq_id < nb)
        def _():
            pltpu.sync_copy(pt_hbm.at[seq_id], pt_v)
            gather_sequence(seq_id, lane, sel_hbm, pt_v, ck_hbm, cv_hbm, kg_hbm, vg_hbm,
                            sel_v, key_v, row_v, buf)

    def gather_sequence(seq_id, lane, sel_hbm, pt_v, ck_hbm, cv_hbm, kg_hbm, vg_hbm,
                        sel_v, key_v, row_v, buf):
        for t in range(t_new):
            q = seq_id * t_new + t
            pltpu.sync_copy(sel_hbm.at[q, pl.ds(0, past)], sel_v)
            for j in range(key_v.shape[0] // SC_LANES):
                key_v[pl.ds(j * SC_LANES, SC_LANES)] = jnp.zeros((SC_LANES,), I32)

            def compact(j, cnt):
                m = sel_v[pl.ds(j * SC_LANES, SC_LANES)] > 0.5
                key = j * SC_LANES + lane
                page = plsc.load_gather(pt_v, [lax.shift_right_logical(key, 7)])
                phys = page * PAGE_SIZE + (key & (PAGE_SIZE - 1))
                plsc.store_compressed(key_v.at[pl.ds(cnt, SC_LANES)], phys, mask=m)
                return cnt + jnp.sum(m.astype(I32))

            lax.fori_loop(0, past // SC_LANES, compact, jnp.int32(0))

            def expand(i, carry):
                r = i * SC_LANES + lane
                k = plsc.load_gather(key_v, [lax.shift_right_logical(r, 3)])
                row_v[pl.ds(i * SC_LANES, SC_LANES)] = k * N_HEADS + (r & (N_HEADS - 1))
                return carry

            lax.fori_loop(0, rows_per_q // SC_LANES, expand, 0)

            for w in range(rows_per_q // GATHER_ROWS):
                idx = row_v.at[pl.ds(w * GATHER_ROWS, GATHER_ROWS)]
                dst = pl.ds(q * rows_per_q + w * GATHER_ROWS, GATHER_ROWS)
                pltpu.sync_copy(ck_hbm.at[idx], buf)
                pltpu.sync_copy(buf, kg_hbm.at[dst])
                pltpu.sync_copy(cv_hbm.at[idx], buf)
                pltpu.sync_copy(buf, vg_hbm.at[dst])

    return pl.kernel(
        body, out_type=(out, out), mesh=mesh,
        scratch_types=[pltpu.VMEM((past,), F32), pltpu.VMEM((n_pages,), I32),
                       pltpu.VMEM((n_sel + SC_LANES,), I32), pltpu.VMEM((rows_per_q,), I32),
                       pltpu.VMEM((GATHER_ROWS, HEAD_DIM), F32)],
        compiler_params=pltpu.CompilerParams(needs_layout_passes=False),
        name="sample_gather",
    )(sel2d, page_table, cache_k_rows, cache_v_rows)


def _gathered_attn_kernel(t_new, n_sel, past, sel_ref, q_ref, kg_ref, vg_ref, kn_ref, vn_ref, a_ref):
    cols = n_sel * N_HEADS
    row = lax.broadcasted_iota(I32, (N_HEADS, cols), 0)
    col = lax.broadcasted_iota(I32, (N_HEADS, cols), 1)
    own_head = (col & (N_HEADS - 1)) == row
    slot = lax.shift_right_logical(col, 3)
    ncol = t_new * N_HEADS
    rown = lax.broadcasted_iota(I32, (N_HEADS, ncol), 0)
    coln = lax.broadcasted_iota(I32, (N_HEADS, ncol), 1)
    own_head_new = (coln & (N_HEADS - 1)) == rown
    new_key = lax.shift_right_logical(coln, 3)

    for t in range(t_new):
        heads = slice(t * N_HEADS, (t + 1) * N_HEADS)
        q = q_ref[0, heads, :]
        sel_past = sel_ref[0, t:t + 1, :past]
        n_past = jnp.sum(sel_past, axis=1, keepdims=True).astype(I32)
        s = jnp.where(own_head & (slot < n_past), _dot_nt(q, kg_ref[t].astype(BF16)), NEG_INF)

        taken = jnp.zeros((N_HEADS, ncol), jnp.bool_)
        for j in range(t_new):
            taken = taken | ((new_key == j) & (sel_ref[0, t:t + 1, past + j:past + j + 1] > 0.5))
        sn = jnp.where(taken & own_head_new, _dot_nt(q, kn_ref[0]), NEG_INF)

        m = jnp.maximum(jnp.max(s, axis=1, keepdims=True), jnp.max(sn, axis=1, keepdims=True))
        p, pn = jnp.exp(s - m), jnp.exp(sn - m)
        l = jnp.sum(p, axis=1, keepdims=True) + jnp.sum(pn, axis=1, keepdims=True)
        o = _dot(p.astype(BF16), vg_ref[t].astype(BF16)) + _dot(pn.astype(BF16), vn_ref[0])
        a_ref[0, heads, :] = o / l


def _gathered_attention(sel3, q_rows, kg, vg, k_new, v_new, *, t_new, n_sel, past):
    nb, _, width = sel3.shape
    cols = n_sel * N_HEADS
    rows = t_new * N_HEADS
    per_seq = lambda r, n: pl.BlockSpec((1, r, n), lambda b: (b, 0, 0))
    gathered = pl.BlockSpec((t_new, cols, HEAD_DIM), lambda b: (b, 0, 0))
    return pl.pallas_call(
        functools.partial(_gathered_attn_kernel, t_new, n_sel, past),
        grid=(nb,),
        in_specs=[per_seq(t_new, width), per_seq(rows, HEAD_DIM), gathered, gathered,
                  per_seq(rows, HEAD_DIM), per_seq(rows, HEAD_DIM)],
        out_specs=per_seq(rows, HEAD_DIM),
        out_shape=jax.ShapeDtypeStruct((nb, rows, HEAD_DIM), F32),
        compiler_params=pltpu.CompilerParams(dimension_semantics=("arbitrary",),
                                             vmem_limit_bytes=VMEM_LIMIT),
        name="gathered_attn",
    )(sel3, q_rows, kg.reshape(nb * t_new, cols, HEAD_DIM), vg.reshape(nb * t_new, cols, HEAD_DIM),
      k_new, v_new)


def _sample_combine_kernel(t_new, a_ref, ga_ref, gb_ref, ext_ref, x_ref,
                           wp_ref, ps_ref, wo_ref, g_ref, b_ref, y_ref):
    p_t = []
    for t in range(t_new):
        parts = []
        for g, w in enumerate(POOL_WINDOWS):
            cols = slice(g * POOL_GROUP_DIM, (g + 1) * POOL_GROUP_DIM)
            cur = POOL_STATE + t
            win = ext_ref[cur, :, cols]
            for j in range(1, w):
                win = win + ext_ref[cur - j, :, cols]
            pooled = win / float(w) - ext_ref[cur, :, cols]
            parts.append(_dot(pooled.astype(BF16), wp_ref[g]))
        p_t.append(jnp.concatenate(parts, axis=-1) * ps_ref[...])
    y_ref[...] = _gate_out_norm(a_ref[...], ga_ref[...], jnp.concatenate(p_t, axis=0), gb_ref[...],
                                x_ref[...], wo_ref, g_ref, b_ref)


def _sample_combine(a, ga, gb, ext_tm, x2d, w_pool_b, pool_scale, w_out_b, ln_g, ln_b, *, t_new):
    return pl.pallas_call(
        functools.partial(_sample_combine_kernel, t_new),
        out_shape=jax.ShapeDtypeStruct(x2d.shape, F32),
        compiler_params=pltpu.CompilerParams(vmem_limit_bytes=VMEM_LIMIT),
        name="sample_combine",
    )(a, ga, gb, ext_tm, x2d, w_pool_b, pool_scale, w_out_b, ln_g, ln_b)


def kernel(x_prompt, x_sample, cache_k, cache_v, cache_kidx, state_pool, page_table,
           w_in, w_pool, pool_scale, w_out, ln_g, ln_b):
    assert w_in.shape[0] == DEPTH == 1
    batch, seq, _ = x_prompt.shape
    nb, t_new, _ = x_sample.shape
    n_pool, n_pages = cache_k.shape[1], page_table.shape[1]
    past = n_pages * PAGE_SIZE

    w_t = jnp.swapaxes(w_in[0], 0, 1).astype(BF16)
    w_pool_b = w_pool[0].astype(BF16)
    w_out_b = w_out[0].astype(BF16)
    ps, g, b = pool_scale[0][None, :], ln_g[0][None, :], ln_b[0][None, :]

    xp2d = x_prompt.reshape(batch * seq, D_MODEL)
    (q, kf, kb, vf, vb, ga, qi_hm, _, kib, kit, wit, u, gb) = _project(
        xp2d, w_t, np.arange(seq), tm=KEY_CHUNK, rows_per_seq=seq, head_major=True)
    a = _prompt_attention(q, qi_hm, wit, kib, kb, vb, batch=batch, seq=seq)
    y_prompt = _combine(a, ga, gb, u, xp2d, w_pool_b, ps, w_out_b, g, b, seq=seq, tm=256)

    rows_s = nb * t_new
    xs2d = x_sample.reshape(rows_s, D_MODEL)
    (qs, kfs, kbs, vfs, vbs, gas, qis, kifs, kibs, _, wits, us, gbs) = _project(
        xs2d, w_t, np.tile(past + np.arange(t_new), nb), tm=rows_s, rows_per_seq=rows_s,
        head_major=False)

    per_seq = lambda z, r, n: z.reshape(nb, r, n)
    pad_rows = lambda z, n: jnp.pad(z, ((0, 0), (0, n - z.shape[1]), (0, 0)))
    scores = _sample_scores(
        page_table,
        per_seq(qis, t_new * IDX_HEADS, IDX_DIM),
        per_seq(wits.T, t_new * IDX_HEADS, 1),
        pad_rows(per_seq(kibs, t_new, IDX_DIM), PAGE_SIZE),
        jnp.swapaxes(cache_kidx[0], 1, 2),
        t_new=t_new)
    width = scores.shape[-1]
    sel = _sample_select(scores.reshape(rows_s, width), topk=min(TOPK_MAX, (past + t_new) // 4))
    head_rows = t_new * N_HEADS
    n_sel = min(TOPK_MAX, (past + t_new) // 4)
    kg, vg = _sample_gather(sel, page_table,
                            cache_k.reshape(n_pool * PAGE_SIZE * N_HEADS, HEAD_DIM),
                            cache_v.reshape(n_pool * PAGE_SIZE * N_HEADS, HEAD_DIM), n_sel=n_sel)
    a_s = _gathered_attention(
        per_seq(sel, t_new, width), per_seq(qs, head_rows, HEAD_DIM), kg, vg,
        per_seq(kbs, head_rows, HEAD_DIM), per_seq(vbs, head_rows, HEAD_DIM),
        t_new=t_new, n_sel=n_sel, past=past).reshape(rows_s, ATTN_WIDTH)

    u_ext = jnp.concatenate([state_pool[0], us.reshape(nb, t_new, POOL_WIDTH)], axis=1)
    time_major = lambda z: z.reshape(nb, t_new, z.shape[-1]).transpose(1, 0, 2).reshape(rows_s, z.shape[-1])
    y_tm = _sample_combine(time_major(a_s), time_major(gas), time_major(gbs), u_ext.transpose(1, 0, 2),
                           time_major(xs2d), w_pool_b, ps, w_out_b, g, b, t_new=t_new)
    y_sample = y_tm.reshape(t_new, nb, D_MODEL).transpose(1, 0, 2)

    hd = (N_HEADS, HEAD_DIM)
    return (y_prompt.reshape(batch, seq, D_MODEL),
            y_sample,
            kf.reshape(1, batch, seq, *hd), vf.reshape(1, batch, seq, *hd),
            jnp.swapaxes(kit, 1, 2)[None],
            u.reshape(batch, seq, POOL_WIDTH)[None, :, -POOL_STATE:],
            kfs.reshape(1, nb, t_new, *hd), vfs.reshape(1, nb, t_new, *hd),
            kifs.reshape(1, nb, t_new, IDX_DIM),
            u_ext[None, :, -POOL_STATE:])
```
